```python
import math
import jax, jax.numpy as jnp
from jax import lax
import numpy as np

D_MODEL = 2048
BATCH = 4
SEQ = 8192
DEPTH = 2

GRID_W = 64
CTX_LEN = 256
NA_HEADS = 8
NA_HEAD_DIM = D_MODEL // 16
NA_W = NA_HEADS * NA_HEAD_DIM
NA_WIN_ROWS = 8
NA_WIN_COLS = 16
CONV_CH = D_MODEL // 2
CONV_WIDTH = 31
DIFF_HEADS = 8
DIFF_QK_DIM = D_MODEL // 32
DIFF_V_DIM = 2 * DIFF_QK_DIM
DIFF_QK_W = DIFF_HEADS * 2 * DIFF_QK_DIM
DIFF_V_W = DIFF_HEADS * DIFF_V_DIM
N_BRANCH = 3
IN_SIZES = (NA_W, NA_W, NA_W, 2 * CONV_CH, DIFF_QK_W, DIFF_QK_W, DIFF_V_W, N_BRANCH * D_MODEL)
IN_COLS = 3 * NA_W + 2 * CONV_CH + 2 * DIFF_QK_W + DIFF_V_W + N_BRANCH * D_MODEL
FFN_DIM = 5632
FFN_CONV_WIDTH = 3
ROPE_THETA = 10000.0
EPS = 1e-6
BLOCK_Q = 128

kernel_name = 'hybrid_na_conformer_diffattn_convffn_dit'

F32 = jnp.float32


def _rmsnorm(x, g):
    x32 = x.astype(F32)
    y = x32 * lax.rsqrt(jnp.mean(x32 * x32, axis=-1, keepdims=True) + EPS)
    return (y * g.astype(F32)).astype(x.dtype)


def _layernorm(x, g, b):
    x32 = x.astype(F32)
    mu = jnp.mean(x32, axis=-1, keepdims=True)
    xc = x32 - mu
    y = xc * lax.rsqrt(jnp.mean(xc * xc, axis=-1, keepdims=True) + EPS)
    return (y * g.astype(F32) + b.astype(F32)).astype(x.dtype)


def _modulate(h, shift, scale):
    return h * (1.0 + scale) + shift


def _heads(t, n):
    b, s, _ = t.shape
    return t.reshape(b, s, n, -1).transpose(0, 2, 1, 3)


def _merge_heads(t):
    b, n, s, d = t.shape
    return t.transpose(0, 2, 1, 3).reshape(b, s, n * d)


def _split_in(p):
    idx, acc = [], 0
    for n in IN_SIZES[:-1]:
        acc += n
        idx.append(acc)
    return jnp.split(p, idx, axis=-1)


def _rope_axis(x, pos):
    d = x.shape[-1]
    inv = ROPE_THETA ** (-jnp.arange(0, d, 2, dtype=F32) / d)
    ang = pos.astype(F32)[:, None] * inv[None, :]
    cos, sin = jnp.cos(ang).astype(x.dtype), jnp.sin(ang).astype(x.dtype)
    x1, x2 = x[..., : d // 2], x[..., d // 2:]
    return jnp.concatenate([x1 * cos - x2 * sin, x1 * sin + x2 * cos], axis=-1)


def _rope_2d(x, rows, cols):
    h = x.shape[-1] // 2
    return jnp.concatenate([_rope_axis(x[..., :h], rows), _rope_axis(x[..., h:], cols)], axis=-1)


def _dwconv(x, w, b):
    k, ch = w.shape
    y = lax.conv_general_dilated(x, w.reshape(k, 1, ch).astype(x.dtype), (1,), [(k // 2, k // 2)],
                                 dimension_numbers=('NWC', 'WIO', 'NWC'), feature_group_count=ch)
    return y + b.astype(x.dtype)


def _attend(q, k, v):
    s = jnp.einsum('bhqd,bhkd->bhqk', q, k, preferred_element_type=F32) * (q.shape[-1] ** -0.5)
    return jnp.einsum('bhqk,bhkd->bhqd', jax.nn.softmax(s, axis=-1).astype(v.dtype), v)


def _na_latent(q, k, v, kc, vc, rpb):
    b, h, s, dh = q.shape
    n_rows = s // GRID_W
    wr = min(NA_WIN_ROWS, n_rows)
    wc = NA_WIN_COLS
    col = np.arange(GRID_W)
    cidx = np.clip(col - wc // 2, 0, GRID_W - wc)[:, None] + np.arange(wc)[None, :]
    dc = cidx - col[:, None] + (NA_WIN_COLS - 1)
    qg = q.reshape(b, h, n_rows, GRID_W, dh)
    kg = k.reshape(b, h, n_rows, GRID_W, dh)
    vg = v.reshape(b, h, n_rows, GRID_W, dh)
    scale = dh ** -0.5

    def row_block(r):
        r0 = jnp.clip(r - wr // 2, 0, n_rows - wr)
        qr = lax.dynamic_index_in_dim(qg, r, axis=2, keepdims=False)
        kw = lax.dynamic_slice_in_dim(kg, r0, wr, axis=2)[:, :, :, cidx]
        vw = lax.dynamic_slice_in_dim(vg, r0, wr, axis=2)[:, :, :, cidx]
        dr = r0 + jnp.arange(wr) - r + (NA_WIN_ROWS - 1)
        bias = rpb[:, dr][:, :, dc].transpose(0, 2, 1, 3).astype(F32)
        s_win = jnp.einsum('bhqd,bhrqkd->bhqrk', qr, kw, preferred_element_type=F32) * scale + bias[None]
        s_ctx = jnp.einsum('bhqd,bhld->bhql', qr, kc, preferred_element_type=F32) * scale
        p = jax.nn.softmax(jnp.concatenate([s_win.reshape(b, h, GRID_W, wr * wc), s_ctx], axis=-1), axis=-1)
        p = p.astype(v.dtype)
        p_win = p[..., : wr * wc].reshape(b, h, GRID_W, wr, wc)
        return (jnp.einsum('bhqrk,bhrqkd->bhqd', p_win, vw)
                + jnp.einsum('bhql,bhld->bhqd', p[..., wr * wc:], vc))

    o = lax.map(row_block, jnp.arange(n_rows))
    return o.transpose(1, 2, 0, 3, 4).reshape(b, h, s, dh)


def _diff_maps(q1, q2, k1, k2, v, lam):
    scale = DIFF_QK_DIM ** -0.5
    a1 = jax.nn.softmax(jnp.einsum('bhqd,bhkd->bhqk', q1, k1, preferred_element_type=F32) * scale, axis=-1)
    a2 = jax.nn.softmax(jnp.einsum('bhqd,bhkd->bhqk', q2, k2, preferred_element_type=F32) * scale, axis=-1)
    return jnp.einsum('bhqk,bhkd->bhqd', (a1 - lam * a2).astype(v.dtype), v)


def _diff_latent(q1, q2, k1, k2, v, lam):
    b, h, s, _ = q1.shape

    def blk(i):
        sl = lambda t: lax.dynamic_slice_in_dim(t, i * BLOCK_Q, BLOCK_Q, axis=2)
        return _diff_maps(sl(q1), sl(q2), k1, k2, v, lam)

    o = lax.map(blk, jnp.arange(s // BLOCK_Q))
    return o.transpose(1, 2, 0, 3, 4).reshape(b, h, s, -1)


def _diff_out(o, g, lam_init):
    return _merge_heads(_rmsnorm(o, g) * (1.0 - lam_init))


def _conformer(u, conv_w, conv_b, ln_g, ln_b):
    a, gt = jnp.split(u, 2, axis=-1)
    y = a * jax.nn.sigmoid(gt)
    y = _dwconv(y, conv_w, conv_b)
    return jax.nn.silu(_layernorm(y, ln_g, ln_b))


def _merge(gate_pre, oa, ob, oc, p_a, p_b, p_c, w_out):
    b, s, _ = gate_pre.shape
    g = jax.nn.sigmoid(gate_pre.astype(F32)).astype(oa.dtype).reshape(b, s, N_BRANCH, D_MODEL)
    y = g[:, :, 0] * (oa @ p_a) + g[:, :, 1] * (ob @ p_b) + g[:, :, 2] * (oc @ p_c)
    return y @ w_out


def _conv_ffn(h, w_up, conv_w, conv_b, w_down):
    u = _dwconv(h @ w_up, conv_w, conv_b)
    a, b = jnp.split(u, 2, axis=-1)
    return (jax.nn.silu(a) * b) @ w_down


def setup_inputs(seed: int = 0) -> dict:
    key = jax.random.key(seed)
    ks = iter(jax.random.split(key, 32))
    L = DEPTH

    def nrm(shape, scale):
        return jax.random.normal(next(ks), shape, F32) * scale

    def gain(shape):
        return 1.0 + nrm(shape, 0.02)

    return {
        'x': nrm((BATCH, SEQ, D_MODEL), 1.0),
        'c': nrm((BATCH, D_MODEL), 1.0),
        'ctx': nrm((BATCH, CTX_LEN, D_MODEL), 1.0),
        'c_ctx': nrm((D_MODEL,), 1.0),
        'w_mod': nrm((L, D_MODEL, 6 * D_MODEL), 0.5 * D_MODEL ** -0.5),
        'b_mod': nrm((L, 6 * D_MODEL), 0.02),
        'g_pre_mix': gain((L, D_MODEL)),
        'w_in': nrm((L, D_MODEL, IN_COLS), D_MODEL ** -0.5),
        'na_rpb': nrm((L, NA_HEADS, 2 * NA_WIN_ROWS - 1, 2 * NA_WIN_COLS - 1), 0.1),
        'conv_w': nrm((L, CONV_WIDTH, CONV_CH), CONV_WIDTH ** -0.5),
        'conv_b': nrm((L, CONV_CH), 0.02),
        'conv_ln_g': gain((L, CONV_CH)),
        'conv_ln_b': nrm((L, CONV_CH), 0.02),
        'lam_q1': nrm((L, DIFF_QK_DIM), 0.1),
        'lam_k1': nrm((L, DIFF_QK_DIM), 0.1),
        'lam_q2': nrm((L, DIFF_QK_DIM), 0.1),
        'lam_k2': nrm((L, DIFF_QK_DIM), 0.1),
        'diff_ln_g': gain((L, DIFF_V_DIM)),
        'p_a': nrm((L, NA_W, D_MODEL), NA_W ** -0.5),
        'p_b': nrm((L, CONV_CH, D_MODEL), CONV_CH ** -0.5),
        'p_c': nrm((L, DIFF_V_W, D_MODEL), DIFF_V_W ** -0.5),
        'w_out': nrm((L, D_MODEL, D_MODEL), D_MODEL ** -0.5),
        'g_post_mix': gain((L, D_MODEL)),
        'g_pre_ffn': gain((L, D_MODEL)),
        'w_up': nrm((L, D_MODEL, 2 * FFN_DIM), D_MODEL ** -0.5),
        'ffn_conv_w': nrm((L, FFN_CONV_WIDTH, 2 * FFN_DIM), FFN_CONV_WIDTH ** -0.5),
        'ffn_conv_b': nrm((L, 2 * FFN_DIM), 0.02),
        'w_down': nrm((L, FFN_DIM, D_MODEL), FFN_DIM ** -0.5),
        'g_post_ffn': gain((L, D_MODEL)),
    }


def reference(x, c, ctx, c_ctx, w_mod, b_mod, g_pre_mix, w_in, na_rpb, conv_w, conv_b, conv_ln_g,
              conv_ln_b, lam_q1, lam_k1, lam_q2, lam_k2, diff_ln_g, p_a, p_b, p_c, w_out, g_post_mix,
              g_pre_ffn, w_up, ffn_conv_w, ffn_conv_b, w_down, g_post_ffn):
    b, s, _ = x.shape
    t = jnp.arange(s)
    rows, cols = t // GRID_W, t % GRID_W
    dqk = DIFF_QK_DIM
    for l in range(DEPTH):
        last = l == DEPTH - 1
        lam_init = 0.8 - 0.6 * math.exp(-0.3 * l)
        lam = (jnp.exp(jnp.sum((lam_q1[l] * lam_k1[l]).astype(F32)))
               - jnp.exp(jnp.sum((lam_q2[l] * lam_k2[l]).astype(F32))) + lam_init)
        mx = jnp.split((jax.nn.silu(c) @ w_mod[l] + b_mod[l])[:, None, :], 6, axis=-1)
        mc = jnp.split((jax.nn.silu(c_ctx) @ w_mod[l] + b_mod[l])[None, None, :], 6, axis=-1)

        hx = _modulate(_rmsnorm(x, g_pre_mix[l]), mx[0], mx[1])
        hc = _modulate(_rmsnorm(ctx, g_pre_mix[l]), mc[0], mc[1])
        nq, nk, nv, u, dq, dk, dv, gt = _split_in(hx @ w_in[l])
        cnq, cnk, cnv, cu, cdq, cdk, cdv, cgt = _split_in(hc @ w_in[l])
        kc_na, vc_na = _heads(cnk, NA_HEADS), _heads(cnv, NA_HEADS)
        dkc, dvc = _heads(cdk, DIFF_HEADS), _heads(cdv, DIFF_HEADS)

        oa = _merge_heads(_na_latent(_heads(nq, NA_HEADS), _heads(nk, NA_HEADS), _heads(nv, NA_HEADS),
                                     kc_na, vc_na, na_rpb[l]))
        ob = _conformer(u, conv_w[l], conv_b[l], conv_ln_g[l], conv_ln_b[l])
        dqh, dkh = _heads(dq, DIFF_HEADS), _heads(dk, DIFF_HEADS)
        q1 = _rope_2d(dqh[..., :dqk], rows, cols)
        q2 = _rope_2d(dqh[..., dqk:], rows, cols)
        k1 = jnp.concatenate([_rope_2d(dkh[..., :dqk], rows, cols), dkc[..., :dqk]], axis=2)
        k2 = jnp.concatenate([_rope_2d(dkh[..., dqk:], rows, cols), dkc[..., dqk:]], axis=2)
        v_all = jnp.concatenate([_heads(dv, DIFF_HEADS), dvc], axis=2)
        oc = _diff_out(_diff_latent(q1, q2, k1, k2, v_all, lam), diff_ln_g[l], lam_init)
        mix_x = _merge(gt, oa, ob, oc, p_a[l], p_b[l], p_c[l], w_out[l])
        if not last:
            dqc = _heads(cdq, DIFF_HEADS)
            oa_c = _merge_heads(_attend(_heads(cnq, NA_HEADS), kc_na, vc_na))
            ob_c = _conformer(cu, conv_w[l], conv_b[l], conv_ln_g[l], conv_ln_b[l])
            oc_c = _diff_out(_diff_maps(dqc[..., :dqk], dqc[..., dqk:], dkc[..., :dqk], dkc[..., dqk:], dvc, lam),
                             diff_ln_g[l], lam_init)
            mix_c = _merge(cgt, oa_c, ob_c, oc_c, p_a[l], p_b[l], p_c[l], w_out[l])
            ctx = ctx + mc[2] * _rmsnorm(mix_c, g_post_mix[l])
        x = x + mx[2] * _rmsnorm(mix_x, g_post_mix[l])

        hx = _modulate(_rmsnorm(x, g_pre_ffn[l]), mx[3], mx[4])
        x = x + mx[5] * _rmsnorm(_conv_ffn(hx, w_up[l], ffn_conv_w[l], ffn_conv_b[l], w_down[l]), g_post_ffn[l])
        if not last:
            hc = _modulate(_rmsnorm(ctx, g_pre_ffn[l]), mc[3], mc[4])
            ctx = ctx + mc[5] * _rmsnorm(_conv_ffn(hc, w_up[l], ffn_conv_w[l], ffn_conv_b[l], w_down[l]),
                                         g_post_ffn[l])
    return x
```

```python
import functools
import math

import numpy as np
import jax
import jax.numpy as jnp
from jax import lax
from jax.experimental import pallas as pl
from jax.experimental.pallas import tpu as pltpu

F32 = jnp.float32
BF16 = jnp.bfloat16

EPS = 1e-6
GRID_W = 64
NA_HEADS = 8
NA_WIN_ROWS = 8
NA_WIN_COLS = 16
DIFF_HEADS = 8
DIFF_QK_DIM = 64
HEAD_W = 128
ROPE_THETA = 10000.0
NEG = -1e30

NA_Q_ROWS = 8
NA_K_ROWS = 16
NA_K_CHUNK = 4
HALO = 16

VMEM_LIMIT = 56 * 1024 * 1024
_NT = (((1,), (1,)), ((), ()))


def _params(sem):
    return pltpu.CompilerParams(dimension_semantics=sem, vmem_limit_bytes=VMEM_LIMIT)


def _sigmoid(x):
    return 1.0 / (1.0 + jnp.exp(-x))


def _mod_kernel(c_ref, w_ref, b_ref, o_ref):
    c = c_ref[...]
    h = (c * _sigmoid(c)).astype(BF16)
    o_ref[0] = jnp.dot(h, w_ref[0].astype(BF16), preferred_element_type=F32) + b_ref[0]


def _modulation(cond, w_mod, b_mod):
    nl, d, n = w_mod.shape
    tn = 1024
    return pl.pallas_call(
        _mod_kernel,
        grid=(nl, n // tn),
        in_specs=[pl.BlockSpec((8, d), lambda l, j: (0, 0)),
                  pl.BlockSpec((1, d, tn), lambda l, j: (l, 0, j)),
                  pl.BlockSpec((1, 1, tn), lambda l, j: (l, 0, j))],
        out_specs=pl.BlockSpec((1, 8, tn), lambda l, j: (l, 0, j)),
        out_shape=jax.ShapeDtypeStruct((nl, 8, n), F32),
        compiler_params=_params(("parallel", "parallel")),
    )(cond, w_mod, b_mod.reshape(nl, 1, n))


def _rope_block(blk, cos, s1, s2):
    return blk * cos + pltpu.roll(blk, HEAD_W - 16, axis=1) * s1 + pltpu.roll(blk, 16, axis=1) * s2


def _nmm_kernel(*refs, q_tile, k_tile, q_scale, rope):
    if rope:
        x_ref, g_ref, sh_ref, sc_ref, w_ref, cos_ref, s1_ref, s2_ref, o_ref, h_ref = refs
    else:
        x_ref, g_ref, sh_ref, sc_ref, w_ref, o_ref, h_ref = refs
    j = pl.program_id(1)

    @pl.when(j == 0)
    def _():
        x = x_ref[...]
        y = x * lax.rsqrt(jnp.mean(x * x, axis=-1, keepdims=True) + EPS) * g_ref[...]
        h_ref[...] = (y * (1.0 + sc_ref[0]) + sh_ref[0]).astype(BF16)

    acc = jnp.dot(h_ref[...], w_ref[...], preferred_element_type=F32)

    def roped(scale):
        for hb in range(acc.shape[1] // HEAD_W):
            blk = acc[:, hb * HEAD_W:(hb + 1) * HEAD_W]
            if rope:
                blk = _rope_block(blk, cos_ref[...], s1_ref[...], s2_ref[...])
            o_ref[:, hb * HEAD_W:(hb + 1) * HEAD_W] = (blk * scale).astype(o_ref.dtype)

    if q_tile is None:
        o_ref[...] = acc.astype(o_ref.dtype)
    else:
        @pl.when(j == q_tile)
        def _():
            roped(q_scale)

        @pl.when(j == k_tile)
        def _():
            roped(1.0)

        @pl.when(jnp.logical_and(j != q_tile, j != k_tile))
        def _():
            o_ref[...] = acc.astype(o_ref.dtype)


def _norm_mod_matmul(x, g, shift, scale, w, seq, tm, tn, *, q_tile=None, k_tile=None,
                     q_scale=1.0, rope_tabs=None):
    n, d = x.shape
    nout = w.shape[1]
    tps = seq // tm
    rope = rope_tabs is not None
    in_specs = [pl.BlockSpec((tm, d), lambda i, j: (i, 0)),
                pl.BlockSpec((1, d), lambda i, j: (0, 0)),
                pl.BlockSpec((1, 1, d), lambda i, j: (i // tps, 0, 0)),
                pl.BlockSpec((1, 1, d), lambda i, j: (i // tps, 0, 0)),
                pl.BlockSpec((d, tn), lambda i, j: (0, j))]
    args = [x, g.reshape(1, d), shift, scale, w]
    if rope:
        in_specs += [pl.BlockSpec((tm, HEAD_W), lambda i, j: (i % tps, 0))] * 3
        args += list(rope_tabs)
    return pl.pallas_call(
        functools.partial(_nmm_kernel, q_tile=q_tile, k_tile=k_tile, q_scale=q_scale, rope=rope),
        grid=(n // tm, nout // tn),
        in_specs=in_specs,
        out_specs=pl.BlockSpec((tm, tn), lambda i, j: (i, j)),
        out_shape=jax.ShapeDtypeStruct((n, nout), BF16),
        scratch_shapes=[pltpu.VMEM((tm, d), BF16)],
        compiler_params=_params(("parallel", "arbitrary")),
    )(*args)


def _rope_tables(seq):
    t = jnp.arange(seq)
    rows, cols = (t // GRID_W).astype(F32), (t % GRID_W).astype(F32)
    half = DIFF_QK_DIM // 2
    inv = ROPE_THETA ** (-jnp.arange(0, half, 2, dtype=F32) / half)
    lane = np.arange(HEAD_W) % DIFF_QK_DIM
    use_col = lane >= half
    e = lane % half
    freq = e % (half // 2)
    second = e >= half // 2
    pos = jnp.where(jnp.asarray(use_col)[None, :], cols[:, None], rows[:, None])
    ang = pos * inv[freq][None, :]
    cos, sin = jnp.cos(ang), jnp.sin(ang)
    s1 = jnp.where(jnp.asarray(second)[None, :], 0.0, -sin)
    s2 = jnp.where(jnp.asarray(second)[None, :], sin, 0.0)
    return cos, s1, s2


def _na_kernel(q_ref, k0, k1, k2, k3, v0, v1, v2, v3, kc_ref, vc_ref, bias_ref, o_ref):
    q = q_ref[...]
    scale = HEAD_W ** -0.5
    ck = NA_K_CHUNK * GRID_W
    s = [lax.dot_general(q, kr[...], _NT, preferred_element_type=F32) * scale
         + bias_ref[0, 0, :, j * ck:(j + 1) * ck] for j, kr in enumerate((k0, k1, k2, k3))]
    s.append(lax.dot_general(q, kc_ref[...], _NT, preferred_element_type=F32) * scale)
    m = s[0].max(axis=-1, keepdims=True)
    for t in s[1:]:
        m = jnp.maximum(m, t.max(axis=-1, keepdims=True))
    p = [jnp.exp(t - m) for t in s]
    l = p[0].sum(axis=-1, keepdims=True)
    for t in p[1:]:
        l = l + t.sum(axis=-1, keepdims=True)
    o = jnp.dot(p[4].astype(BF16), vc_ref[...], preferred_element_type=F32)
    for t, vr in zip(p[:4], (v0, v1, v2, v3)):
        o = o + jnp.dot(t.astype(BF16), vr[...], preferred_element_type=F32)
    o_ref[...] = (o / l).astype(o_ref.dtype)


def _na_bias_table(rpb):
    i = np.arange(NA_Q_ROWS)[:, None]
    jr = np.arange(NA_K_ROWS)[None, :]
    half = NA_WIN_ROWS // 2
    lo_top = np.maximum(i - half, 0)
    lo_int = i + 0 * jr
    lo_bot = NA_Q_ROWS + np.minimum(i - half, 0)
    offs = (0, -half, -NA_Q_ROWS)
    dr, rv = [], []
    for off, lo in zip(offs, (lo_top, lo_int, lo_bot)):
        dr.append(off + jr - i + NA_WIN_ROWS - 1)
        rv.append((jr >= lo) & (jr < lo + NA_WIN_ROWS))
    dr, rv = np.stack(dr), np.stack(rv)
    qc = np.arange(GRID_W)[:, None]
    kc = np.arange(GRID_W)[None, :]
    c0 = np.clip(qc - NA_WIN_COLS // 2, 0, GRID_W - NA_WIN_COLS)
    dc = kc - qc + NA_WIN_COLS - 1
    cv = (kc >= c0) & (kc < c0 + NA_WIN_COLS)
    drc = np.clip(dr, 0, 2 * NA_WIN_ROWS - 2)
    dcc = np.clip(dc, 0, 2 * NA_WIN_COLS - 2)
    t = rpb.astype(F32)[:, drc]
    t = t[..., dcc]
    valid = rv[:, :, :, None, None] & cv[None, None, None]
    t = jnp.where(jnp.asarray(valid)[None], t, NEG)
    t = t.transpose(1, 0, 2, 4, 3, 5)
    h = rpb.shape[0]
    return t.reshape(3, h, NA_Q_ROWS * GRID_W, NA_K_ROWS * GRID_W)


def _na_latent(proj, cproj, bias, b, seq, ctx_len, q_col, k_col, v_col):
    tq = NA_Q_ROWS * GRID_W
    ck = NA_K_CHUNK * GRID_W
    n_rb = seq // tq
    n_ck = seq // ck
    assert seq % tq == 0 and n_rb >= 2
    n_chunks = NA_K_ROWS // NA_K_CHUNK

    def chunk0(rb):
        return jnp.clip(2 * rb - 1, 0, n_ck - n_chunks)

    def kv_spec(col, j):
        return pl.BlockSpec((ck, HEAD_W), lambda h, rb, bi: (bi * n_ck + chunk0(rb) + j, col + h))

    def variant(rb):
        return jnp.where(rb == 0, 0, jnp.where(rb == n_rb - 1, 2, 1))

    in_specs = ([pl.BlockSpec((tq, HEAD_W), lambda h, rb, bi: (bi * n_rb + rb, q_col + h))]
                + [kv_spec(k_col, j) for j in range(n_chunks)]
                + [kv_spec(v_col, j) for j in range(n_chunks)]
                + [pl.BlockSpec((ctx_len, HEAD_W), lambda h, rb, bi: (bi, k_col + h)),
                   pl.BlockSpec((ctx_len, HEAD_W), lambda h, rb, bi: (bi, v_col + h)),
                   pl.BlockSpec((1, 1, tq, NA_K_ROWS * GRID_W), lambda h, rb, bi: (variant(rb), h, 0, 0))])
    return pl.pallas_call(
        _na_kernel,
        grid=(NA_HEADS, n_rb, b),
        in_specs=in_specs,
        out_specs=pl.BlockSpec((tq, HEAD_W), lambda h, rb, bi: (bi * n_rb + rb, h)),
        out_shape=jax.ShapeDtypeStruct((b * seq, NA_HEADS * HEAD_W), BF16),
        compiler_params=_params(("parallel", "parallel", "parallel")),
    )(*([proj] * (1 + 2 * n_chunks) + [cproj, cproj, bias]))


def _dense_attn_kernel(q_ref, k_ref, v_ref, o_ref):
    s = lax.dot_general(q_ref[...], k_ref[...], _NT, preferred_element_type=F32) * (HEAD_W ** -0.5)
    p = jnp.exp(s - s.max(axis=-1, keepdims=True))
    l = p.sum(axis=-1, keepdims=True)
    o = jnp.dot(p.astype(BF16), v_ref[...], preferred_element_type=F32)
    o_ref[...] = (o / l).astype(o_ref.dtype)


def _dense_attn(cproj, b, ctx_len, q_col, k_col, v_col):
    def spec(col):
        return pl.BlockSpec((ctx_len, HEAD_W), lambda bi, h: (bi, col + h))
    return pl.pallas_call(
        _dense_attn_kernel,
        grid=(b, NA_HEADS),
        in_specs=[spec(q_col), spec(k_col), spec(v_col)],
        out_specs=pl.BlockSpec((ctx_len, HEAD_W), lambda bi, h: (bi, h)),
        out_shape=jax.ShapeDtypeStruct((b * ctx_len, NA_HEADS * HEAD_W), BF16),
        compiler_params=_params(("parallel", "parallel")),
    )(cproj, cproj, cproj)


def _conformer_kernel(a_ref, g_ref, pa_ref, pg_ref, na_ref, ng_ref, w_ref, b_ref, lg_ref, lb_ref,
                      o_ref, ext_ref, *, tps, chunk):
    tm = a_ref.shape[0]
    kw = w_ref.shape[0]
    t = pl.program_id(0) % tps

    def glu(a, g):
        return a[...].astype(F32) * _sigmoid(g[...].astype(F32))

    ext_ref[HALO:HALO + tm, :] = glu(a_ref, g_ref)
    ext_ref[0:HALO, :] = jnp.where(t == 0, 0.0, glu(pa_ref, pg_ref))
    ext_ref[HALO + tm:, :] = jnp.where(t == tps - 1, 0.0, glu(na_ref, ng_ref))
    off = HALO - kw // 2
    for r in range(tm // chunk):
        acc = jnp.zeros((chunk, a_ref.shape[1]), F32) + b_ref[...]
        for k in range(kw):
            acc = acc + ext_ref[r * chunk + off + k:r * chunk + off + k + chunk, :] * w_ref[k:k + 1, :]
        mu = jnp.mean(acc, axis=-1, keepdims=True)
        xc = acc - mu
        y = xc * lax.rsqrt(jnp.mean(xc * xc, axis=-1, keepdims=True) + EPS) * lg_ref[...] + lb_ref[...]
        o_ref[r * chunk:(r + 1) * chunk, :] = (y * _sigmoid(y)).astype(o_ref.dtype)


def _conformer(proj, conv_w, conv_b, ln_g, ln_b, seq, tm, a_col, g_col):
    n = proj.shape[0]
    kw, ch = conv_w.shape
    assert kw // 2 < HALO and seq % tm == 0 and tm % HALO == 0
    tps = seq // tm
    r = tm // HALO
    last = n // HALO - 1

    def main(col):
        return pl.BlockSpec((tm, ch), lambda i: (i, col))

    def prev(col):
        return pl.BlockSpec((HALO, ch), lambda i: (jnp.maximum(i * r - 1, 0), col))

    def nxt(col):
        return pl.BlockSpec((HALO, ch), lambda i: (jnp.minimum((i + 1) * r, last), col))

    vec = pl.BlockSpec((1, ch), lambda i: (0, 0))
    return pl.pallas_call(
        functools.partial(_conformer_kernel, tps=tps, chunk=32),
        grid=(n // tm,),
        in_specs=[main(a_col), main(g_col), prev(a_col), prev(g_col), nxt(a_col), nxt(g_col),
                  pl.BlockSpec((kw, ch), lambda i: (0, 0)), vec, vec, vec],
        out_specs=pl.BlockSpec((tm, ch), lambda i: (i, 0)),
        out_shape=jax.ShapeDtypeStruct((n, ch), BF16),
        scratch_shapes=[pltpu.VMEM((tm + 2 * HALO, ch), F32)],
        compiler_params=_params(("parallel",)),
    )(proj, proj, proj, proj, proj, proj, conv_w, conv_b.reshape(1, ch), ln_g.reshape(1, ch),
      ln_b.reshape(1, ch))


def _diff_kernel(q_ref, k_ref, vt_ref, lq1_ref, lk1_ref, lq2_ref, lk2_ref, g_ref, o_ref,
                 m_ref, l_ref, acc_ref, *, lam_init):
    tq = q_ref.shape[0]
    nkb = k_ref.shape[1]
    q = q_ref[...]
    lane = lax.broadcasted_iota(jnp.int32, q.shape, 1)
    zero = jnp.zeros_like(q)
    qbd = jnp.concatenate([jnp.where(lane < DIFF_QK_DIM, q, zero),
                           jnp.where(lane >= DIFF_QK_DIM, q, zero)], axis=0)
    m_ref[...] = jnp.full(m_ref.shape, NEG, F32)
    l_ref[...] = jnp.zeros(l_ref.shape, F32)
    acc_ref[...] = jnp.zeros(acc_ref.shape, F32)

    def body(kb, carry):
        s = lax.dot_general(k_ref[0, kb], qbd, _NT, preferred_element_type=F32)
        m_old = m_ref[...]
        m_new = jnp.maximum(m_old, s.max(axis=0, keepdims=True))
        alpha = jnp.exp(m_old - m_new)
        p = jnp.exp(s - m_new)
        l_ref[...] = alpha * l_ref[...] + p.sum(axis=0, keepdims=True)
        acc_ref[...] = alpha * acc_ref[...] + jnp.dot(vt_ref[0, 0, kb], p.astype(BF16),
                                                      preferred_element_type=F32)
        m_ref[...] = m_new
        return carry

    lax.fori_loop(0, nkb, body, 0)

    lam = (jnp.exp(jnp.sum(lq1_ref[...] * lk1_ref[...], axis=-1, keepdims=True))
           - jnp.exp(jnp.sum(lq2_ref[...] * lk2_ref[...], axis=-1, keepdims=True)) + lam_init)
    o = acc_ref[...] / l_ref[...]
    ot = o[:, :tq] - lam * o[:, tq:]
    y = ot * lax.rsqrt(jnp.mean(ot * ot, axis=0, keepdims=True) + EPS) * g_ref[...]
    o_ref[...] = (y * (1.0 - lam_init)).T.astype(o_ref.dtype)


def _diff_attn(q_arr, q_col, k_all, vt_all, lams, ln_g, lam_init, b, seq, tq):
    nq = seq // tq
    nkb, tk = k_all.shape[1], k_all.shape[2]
    vec = pl.BlockSpec((1, DIFF_QK_DIM), lambda bi, h, qi: (0, 0))
    return pl.pallas_call(
        functools.partial(_diff_kernel, lam_init=lam_init),
        grid=(b, DIFF_HEADS, nq),
        in_specs=[pl.BlockSpec((tq, HEAD_W), lambda bi, h, qi: (bi * nq + qi, q_col + h)),
                  pl.BlockSpec((1, nkb, tk, HEAD_W), lambda bi, h, qi: (bi, 0, 0, h)),
                  pl.BlockSpec((1, 1, nkb, HEAD_W, tk), lambda bi, h, qi: (bi, h, 0, 0, 0)),
                  vec, vec, vec, vec,
                  pl.BlockSpec((HEAD_W, 1), lambda bi, h, qi: (0, 0))],
        out_specs=pl.BlockSpec((tq, HEAD_W), lambda bi, h, qi: (bi * nq + qi, h)),
        out_shape=jax.ShapeDtypeStruct((b * seq, DIFF_HEADS * HEAD_W), BF16),
        scratch_shapes=[pltpu.VMEM((1, 2 * tq), F32), pltpu.VMEM((1, 2 * tq), F32),
                        pltpu.VMEM((HEAD_W, 2 * tq), F32)],
        compiler_params=_params(("parallel", "parallel", "parallel")),
    )(q_arr, k_all, vt_all, *[v.reshape(1, DIFF_QK_DIM) for v in lams], ln_g.reshape(HEAD_W, 1))


def _kv_layout(k_rows, v_rows, b, tk):
    lk = k_rows.shape[1]
    nkb = lk // tk
    k_all = k_rows.reshape(b, nkb, tk, DIFF_HEADS * HEAD_W)
    vt_all = v_rows.reshape(b, nkb, tk, DIFF_HEADS, HEAD_W).transpose(0, 3, 1, 4, 2)
    return k_all, vt_all


def _merge_kernel(oa_ref, ob_ref, oc_ref, g0_ref, g1_ref, g2_ref, pa_ref, pb_ref, pc_ref, wo_ref,
                  gp_ref, gate_ref, x_ref, o_ref):
    def branch(o_r, g_r, p_r):
        return _sigmoid(g_r[...].astype(F32)) * jnp.dot(o_r[...], p_r[...], preferred_element_type=F32)

    y = branch(oa_ref, g0_ref, pa_ref) + branch(ob_ref, g1_ref, pb_ref) + branch(oc_ref, g2_ref, pc_ref)
    mix = jnp.dot(y.astype(BF16), wo_ref[...], preferred_element_type=F32)
    r = mix * lax.rsqrt(jnp.mean(mix * mix, axis=-1, keepdims=True) + EPS) * gp_ref[...]
    o_ref[...] = x_ref[...] + gate_ref[0] * r


def _resident(shape):
    return pl.BlockSpec(shape, lambda i: (0,) * len(shape), pipeline_mode=pl.Buffered(1))


def _merge_out(oa, ob, oc, proj, gate_col, p_a, p_b, p_c, w_out, g_post, gate, x, seq, tm):
    n, d = x.shape
    w = oa.shape[1]
    tps = seq // tm

    def act():
        return pl.BlockSpec((tm, w), lambda i: (i, 0))

    def gcol(k):
        return pl.BlockSpec((tm, d), lambda i: (i, gate_col + k))

    return pl.pallas_call(
        _merge_kernel,
        grid=(n // tm,),
        in_specs=[act(), act(), act(), gcol(0), gcol(1), gcol(2),
                  _resident((w, d)), _resident((w, d)), _resident((w, d)), _resident((d, d)),
                  pl.BlockSpec((1, d), lambda i: (0, 0)),
                  pl.BlockSpec((1, 1, d), lambda i: (i // tps, 0, 0)),
                  pl.BlockSpec((tm, d), lambda i: (i, 0))],
        out_specs=pl.BlockSpec((tm, d), lambda i: (i, 0)),
        out_shape=jax.ShapeDtypeStruct((n, d), F32),
        compiler_params=_params(("parallel",)),
    )(oa, ob, oc, proj, proj, proj, p_a, p_b, p_c, w_out, g_post.reshape(1, d), gate, x)


def _ffn_gate_kernel(a_ref, b_ref, pa_ref, pb_ref, na_ref, nb_ref, wa_ref, wb_ref, ba_ref, bb_ref,
                     o_ref, *, tps):
    tm = a_ref.shape[0]
    t = pl.program_id(0) % tps
    row = lax.broadcasted_iota(jnp.int32, a_ref.shape, 0)

    def conv(x_ref, p_ref, n_ref, w_ref, bias_ref):
        x = x_ref[...].astype(F32)
        before = jnp.where(t == 0, 0.0, p_ref[HALO - 1:HALO, :].astype(F32))
        after = jnp.where(t == tps - 1, 0.0, n_ref[0:1, :].astype(F32))
        xm = jnp.where(row == 0, before, pltpu.roll(x, 1, axis=0))
        xp = jnp.where(row == tm - 1, after, pltpu.roll(x, tm - 1, axis=0))
        return xm * w_ref[0:1, :] + x * w_ref[1:2, :] + xp * w_ref[2:3, :] + bias_ref[...]

    a = conv(a_ref, pa_ref, na_ref, wa_ref, ba_ref)
    b = conv(b_ref, pb_ref, nb_ref, wb_ref, bb_ref)
    o_ref[...] = (a * _sigmoid(a) * b).astype(o_ref.dtype)


def _ffn_gate(u, conv_w, conv_b, seq, tm, tc):
    n, f2 = u.shape
    f = f2 // 2
    kw = conv_w.shape[0]
    assert kw == 3 and f % tc == 0
    nc = f // tc
    tps = seq // tm
    r = tm // HALO
    last = n // HALO - 1

    def main(o):
        return pl.BlockSpec((tm, tc), lambda i, j: (i, j + o))

    def prev(o):
        return pl.BlockSpec((HALO, tc), lambda i, j: (jnp.maximum(i * r - 1, 0), j + o))

    def nxt(o):
        return pl.BlockSpec((HALO, tc), lambda i, j: (jnp.minimum((i + 1) * r, last), j + o))

    def wspec(o):
        return pl.BlockSpec((kw, tc), lambda i, j: (0, j + o))

    def bspec(o):
        return pl.BlockSpec((1, tc), lambda i, j: (0, j + o))

    cb = conv_b.reshape(1, f2)
    return pl.pallas_call(
        functools.partial(_ffn_gate_kernel, tps=tps),
        grid=(n // tm, nc),
        in_specs=[main(0), main(nc), prev(0), prev(nc), nxt(0), nxt(nc),
                  wspec(0), wspec(nc), bspec(0), bspec(nc)],
        out_specs=pl.BlockSpec((tm, tc), lambda i, j: (i, j)),
        out_shape=jax.ShapeDtypeStruct((n, f), BF16),
        compiler_params=_params(("parallel", "parallel")),
    )(u, u, u, u, u, u, conv_w, conv_w, cb, cb)


def _down_kernel(h_ref, w_ref, gp_ref, gate_ref, x_ref, o_ref):
    y = jnp.dot(h_ref[...], w_ref[...], preferred_element_type=F32)
    r = y * lax.rsqrt(jnp.mean(y * y, axis=-1, keepdims=True) + EPS) * gp_ref[...]
    o_ref[...] = x_ref[...] + gate_ref[0] * r


def _down_norm_res(h, w, g_post, gate, x, seq, tm):
    n, d = x.shape
    f = h.shape[1]
    tps = seq // tm
    return pl.pallas_call(
        _down_kernel,
        grid=(n // tm,),
        in_specs=[pl.BlockSpec((tm, f), lambda i: (i, 0)),
                  _resident((f, d)),
                  pl.BlockSpec((1, d), lambda i: (0, 0)),
                  pl.BlockSpec((1, 1, d), lambda i: (i // tps, 0, 0)),
                  pl.BlockSpec((tm, d), lambda i: (i, 0))],
        out_specs=pl.BlockSpec((tm, d), lambda i: (i, 0)),
        out_shape=jax.ShapeDtypeStruct((n, d), F32),
        compiler_params=_params(("parallel",)),
    )(h, w, g_post.reshape(1, d), gate, x)


def kernel(x, c, ctx, c_ctx, w_mod, b_mod, g_pre_mix, w_in, na_rpb, conv_w, conv_b, conv_ln_g,
           conv_ln_b, lam_q1, lam_k1, lam_q2, lam_k2, diff_ln_g, p_a, p_b, p_c, w_out, g_post_mix,
           g_pre_ffn, w_up, ffn_conv_w, ffn_conv_b, w_down, g_post_ffn):
    b, seq, d = x.shape
    ctx_len = ctx.shape[1]
    depth = w_in.shape[0]
    na_w = NA_HEADS * HEAD_W
    conv_ch = conv_w.shape[2]
    qk_w = DIFF_HEADS * 2 * DIFF_QK_DIM
    na_q_col, na_k_col, na_v_col = 0, NA_HEADS, 2 * NA_HEADS
    u_off = 3 * na_w
    dq_off = u_off + 2 * conv_ch
    dk_off = dq_off + qk_w
    dv_off = dk_off + qk_w
    gate_off = dv_off + DIFF_HEADS * HEAD_W
    tn = 1024
    assert dq_off % tn == 0 and qk_w == tn and gate_off % d == 0 and u_off % conv_ch == 0

    xs = x.reshape(b * seq, d)
    cs = ctx.reshape(b * ctx_len, d)
    cond = jnp.concatenate([c, c_ctx[None, :], jnp.zeros((8 - b - 1, d), F32)], axis=0)
    mods = _modulation(cond, w_mod, b_mod)
    rope_tabs = _rope_tables(seq)
    tk = 256
    tm_x, tm_c = 1024, ctx_len

    for l in range(depth):
        last = l == depth - 1
        lam_init = 0.8 - 0.6 * math.exp(-0.3 * l)
        mx = mods[l, :b].reshape(b, 6, 1, d)
        mc = jnp.broadcast_to(mods[l, b].reshape(1, 6, 1, d), (b, 6, 1, d))
        w_in_l = w_in[l].astype(BF16)
        lams = (lam_q1[l], lam_k1[l], lam_q2[l], lam_k2[l])
        pa_l, pb_l, pc_l, wo_l = (p_a[l].astype(BF16), p_b[l].astype(BF16), p_c[l].astype(BF16),
                                  w_out[l].astype(BF16))
        w_up_l, w_down_l = w_up[l].astype(BF16), w_down[l].astype(BF16)
        qscale = DIFF_QK_DIM ** -0.5

        proj = _norm_mod_matmul(xs, g_pre_mix[l], mx[:, 0], mx[:, 1], w_in_l, seq, tm_x, tn,
                                q_tile=dq_off // tn, k_tile=dk_off // tn, q_scale=qscale,
                                rope_tabs=rope_tabs)
        cproj = _norm_mod_matmul(cs, g_pre_mix[l], mc[:, 0], mc[:, 1], w_in_l, ctx_len, tm_c, tn,
                                 q_tile=dq_off // tn, k_tile=dk_off // tn, q_scale=qscale)
        bias = _na_bias_table(na_rpb[l])
        oa = _na_latent(proj, cproj, bias, b, seq, ctx_len, na_q_col, na_k_col, na_v_col)
        ob = _conformer(proj, conv_w[l], conv_b[l], conv_ln_g[l], conv_ln_b[l], seq, 256,
                        u_off // conv_ch, u_off // conv_ch + 1)
        ck = cproj[:, dk_off:dk_off + qk_w].reshape(b, ctx_len, qk_w)
        cv = cproj[:, dv_off:dv_off + qk_w].reshape(b, ctx_len, qk_w)
        k_rows = jnp.concatenate([proj[:, dk_off:dk_off + qk_w].reshape(b, seq, qk_w), ck], axis=1)
        v_rows = jnp.concatenate([proj[:, dv_off:dv_off + qk_w].reshape(b, seq, qk_w), cv], axis=1)
        k_all, vt_all = _kv_layout(k_rows, v_rows, b, tk)
        oc = _diff_attn(proj, dq_off // HEAD_W, k_all, vt_all, lams, diff_ln_g[l], lam_init, b, seq, 256)
        if not last:
            oa_c = _dense_attn(cproj, b, ctx_len, na_q_col, na_k_col, na_v_col)
            ob_c = _conformer(cproj, conv_w[l], conv_b[l], conv_ln_g[l], conv_ln_b[l], ctx_len,
                              ctx_len, u_off // conv_ch, u_off // conv_ch + 1)
            kc_all, vtc_all = _kv_layout(ck, cv, b, tk)
            oc_c = _diff_attn(cproj, dq_off // HEAD_W, kc_all, vtc_all, lams, diff_ln_g[l], lam_init,
                              b, ctx_len, 256)
            cs = _merge_out(oa_c, ob_c, oc_c, cproj, gate_off // d, pa_l, pb_l, pc_l, wo_l,
                            g_post_mix[l], mc[:, 2], cs, ctx_len, 256)
        xs = _merge_out(oa, ob, oc, proj, gate_off // d, pa_l, pb_l, pc_l, wo_l, g_post_mix[l],
                        mx[:, 2], xs, seq, 256)

        u = _norm_mod_matmul(xs, g_pre_ffn[l], mx[:, 3], mx[:, 4], w_up_l, seq, tm_x, tn)
        hg = _ffn_gate(u, ffn_conv_w[l], ffn_conv_b[l], seq, 512, 512)
        xs = _down_norm_res(hg, w_down_l, g_post_ffn[l], mx[:, 5], xs, seq, 256)
        if not last:
            u = _norm_mod_matmul(cs, g_pre_ffn[l], mc[:, 3], mc[:, 4], w_up_l, ctx_len, tm_c, tn)
            hg = _ffn_gate(u, ffn_conv_w[l], ffn_conv_b[l], ctx_len, ctx_len, 512)
            cs = _down_norm_res(hg, w_down_l, g_post_ffn[l], mc[:, 5], cs, ctx_len, 256)
    return xs.reshape(b, seq, d)
```

```python
import functools
import math

import numpy as np
import jax
import jax.numpy as jnp
from jax import lax
from jax.experimental import pallas as pl
from jax.experimental.pallas import tpu as pltpu

F32 = jnp.float32
BF16 = jnp.bfloat16

EPS = 1e-6
GRID_W = 64
NA_HEADS = 8
NA_WIN_ROWS = 8
NA_WIN_COLS = 16
DIFF_HEADS = 8
DIFF_QK_DIM = 64
HEAD_W = 128
SUBLANES = 8
ROPE_THETA = 10000.0
NEG = -1e30

NA_Q_ROWS = 8
NA_K_ROWS = 16
NA_K_CHUNK = 4
HALO = 16

NMM_M_SPLIT = 512

VMEM_LIMIT = 56 * 1024 * 1024
_NT = (((1,), (1,)), ((), ()))


def _params(sem):
    return pltpu.CompilerParams(dimension_semantics=sem, vmem_limit_bytes=VMEM_LIMIT)


def _sigmoid(x):
    return 1.0 / (1.0 + jnp.exp(-x))


def _mod_kernel(c_ref, w_ref, b_ref, o_ref):
    c = c_ref[...]
    h = (c * _sigmoid(c)).astype(BF16)
    o_ref[0] = jnp.dot(h, w_ref[0].astype(BF16), preferred_element_type=F32) + b_ref[0]


def _modulation(cond, w_mod, b_mod):
    nl, d, n = w_mod.shape
    tn = 1024
    return pl.pallas_call(
        _mod_kernel,
        grid=(nl, n // tn),
        in_specs=[pl.BlockSpec((8, d), lambda l, j: (0, 0)),
                  pl.BlockSpec((1, d, tn), lambda l, j: (l, 0, j)),
                  pl.BlockSpec((1, 1, tn), lambda l, j: (l, 0, j))],
        out_specs=pl.BlockSpec((1, 8, tn), lambda l, j: (l, 0, j)),
        out_shape=jax.ShapeDtypeStruct((nl, 8, n), F32),
        compiler_params=_params(("parallel", "parallel")),
    )(cond, w_mod, b_mod.reshape(nl, 1, n))


def _rope_block(blk, cos, s1, s2):
    return blk * cos + pltpu.roll(blk, HEAD_W - 16, axis=1) * s1 + pltpu.roll(blk, 16, axis=1) * s2


def _nmm_kernel(*refs, rope, scaled, m_split):
    x_ref, g_ref, sh_ref, sc_ref, w_ref = refs[:5]
    o_ref, h_ref = refs[-2:]
    extra = list(refs[5:-2])
    cs_ref = extra.pop(0) if scaled else None
    cos_ref, s1_ref, s2_ref = extra if rope else (None, None, None)

    @pl.when(pl.program_id(1) == 0)
    def _():
        x = x_ref[...]
        y = x * lax.rsqrt(jnp.mean(x * x, axis=-1, keepdims=True) + EPS) * g_ref[...]
        h_ref[...] = (y * (1.0 + sc_ref[0]) + sh_ref[0]).astype(BF16)

    tm, tn = o_ref.shape
    for mi in range(tm // m_split):
        rows = slice(mi * m_split, (mi + 1) * m_split)
        acc = jnp.dot(h_ref[rows, :], w_ref[...], preferred_element_type=F32)
        if not (rope or scaled):
            o_ref[rows, :] = acc.astype(o_ref.dtype)
            continue
        for hb in range(tn // HEAD_W):
            cols = slice(hb * HEAD_W, (hb + 1) * HEAD_W)
            blk = acc[:, cols]
            if rope:
                blk = _rope_block(blk, cos_ref[rows, :], s1_ref[rows, :], s2_ref[rows, :])
            if scaled:
                blk = blk * cs_ref[:, cols]
            o_ref[rows, cols] = blk.astype(o_ref.dtype)


def _norm_mod_matmul(x, g, shift, scale, w, seq, tm, tn, *, col_scale=None, rope_tabs=None):
    n, d = x.shape
    nout = w.shape[1]
    tps = seq // tm
    rope = rope_tabs is not None
    scaled = col_scale is not None
    in_specs = [pl.BlockSpec((tm, d), lambda i, j: (i, 0)),
                pl.BlockSpec((1, d), lambda i, j: (0, 0)),
                pl.BlockSpec((1, 1, d), lambda i, j: (i // tps, 0, 0)),
                pl.BlockSpec((1, 1, d), lambda i, j: (i // tps, 0, 0)),
                pl.BlockSpec((d, tn), lambda i, j: (0, j))]
    args = [x, g.reshape(1, d), shift, scale, w]
    if scaled:
        in_specs.append(pl.BlockSpec((1, tn), lambda i, j: (0, j)))
        args.append(col_scale.reshape(1, nout))
    if rope:
        in_specs += [pl.BlockSpec((tm, HEAD_W), lambda i, j: (i % tps, 0))] * 3
        args += list(rope_tabs)
    return pl.pallas_call(
        functools.partial(_nmm_kernel, rope=rope, scaled=scaled, m_split=min(tm, NMM_M_SPLIT)),
        grid=(n // tm, nout // tn),
        in_specs=in_specs,
        out_specs=pl.BlockSpec((tm, tn), lambda i, j: (i, j)),
        out_shape=jax.ShapeDtypeStruct((n, nout), BF16),
        scratch_shapes=[pltpu.VMEM((tm, d), BF16)],
        compiler_params=_params(("parallel", "arbitrary")),
    )(*args)


def _rope_tables(seq):
    t = jnp.arange(seq)
    rows, cols = (t // GRID_W).astype(F32), (t % GRID_W).astype(F32)
    half = DIFF_QK_DIM // 2
    inv = ROPE_THETA ** (-jnp.arange(0, half, 2, dtype=F32) / half)
    lane = np.arange(HEAD_W) % DIFF_QK_DIM
    use_col = lane >= half
    e = lane % half
    freq = e % (half // 2)
    second = e >= half // 2
    pos = jnp.where(jnp.asarray(use_col)[None, :], cols[:, None], rows[:, None])
    ang = pos * inv[freq][None, :]
    cos, sin = jnp.cos(ang), jnp.sin(ang)
    s1 = jnp.where(jnp.asarray(second)[None, :], 0.0, -sin)
    s2 = jnp.where(jnp.asarray(second)[None, :], sin, 0.0)
    return cos, s1, s2


def _na_kernel(q_ref, k0, k1, k2, k3, v0, v1, v2, v3, kc_ref, vc_ref, bias_ref, o_ref):
    q = q_ref[...]
    c = HEAD_W ** -0.5 * math.log2(math.e)
    ck = NA_K_CHUNK * GRID_W
    s = [lax.dot_general(kr[...], q, _NT, preferred_element_type=F32) * c
         + bias_ref[0, 0, j * ck:(j + 1) * ck, :] for j, kr in enumerate((k0, k1, k2, k3))]
    s.append(lax.dot_general(kc_ref[...], q, _NT, preferred_element_type=F32) * c)
    m = s[0].max(axis=0, keepdims=True)
    for t in s[1:]:
        m = jnp.maximum(m, t.max(axis=0, keepdims=True))
    p = [jnp.exp2(t - m) for t in s]
    l = p[0].sum(axis=0, keepdims=True)
    for t in p[1:]:
        l = l + t.sum(axis=0, keepdims=True)
    ot = jnp.dot(vc_ref[0, 0], p[4].astype(BF16), preferred_element_type=F32)
    for t, vr in zip(p[:4], (v0, v1, v2, v3)):
        ot = ot + jnp.dot(vr[0, 0, 0], t.astype(BF16), preferred_element_type=F32)
    o_ref[...] = (ot / l).T.astype(o_ref.dtype)


def _na_bias_table(rpb):
    i = np.arange(NA_Q_ROWS)[:, None]
    jr = np.arange(NA_K_ROWS)[None, :]
    half = NA_WIN_ROWS // 2
    lo_top = np.maximum(i - half, 0)
    lo_int = i + 0 * jr
    lo_bot = NA_Q_ROWS + np.minimum(i - half, 0)
    offs = (0, -half, -NA_Q_ROWS)
    dr, rv = [], []
    for off, lo in zip(offs, (lo_top, lo_int, lo_bot)):
        dr.append(off + jr - i + NA_WIN_ROWS - 1)
        rv.append((jr >= lo) & (jr < lo + NA_WIN_ROWS))
    dr, rv = np.stack(dr), np.stack(rv)
    qc = np.arange(GRID_W)[:, None]
    kc = np.arange(GRID_W)[None, :]
    c0 = np.clip(qc - NA_WIN_COLS // 2, 0, GRID_W - NA_WIN_COLS)
    dc = kc - qc + NA_WIN_COLS - 1
    cv = (kc >= c0) & (kc < c0 + NA_WIN_COLS)
    drc = np.clip(dr, 0, 2 * NA_WIN_ROWS - 2)
    dcc = np.clip(dc, 0, 2 * NA_WIN_COLS - 2)
    t = rpb.astype(F32)[:, drc]
    t = t[..., dcc]
    valid = rv[:, :, :, None, None] & cv[None, None, None]
    t = jnp.where(jnp.asarray(valid)[None], t * math.log2(math.e), NEG)
    t = t.transpose(1, 0, 3, 5, 2, 4)
    h = rpb.shape[0]
    return t.reshape(3, h, NA_K_ROWS * GRID_W, NA_Q_ROWS * GRID_W)


def _na_latent(proj, cproj, vt, vct, bias, b, seq, ctx_len, q_col, k_col):
    tq = NA_Q_ROWS * GRID_W
    ck = NA_K_CHUNK * GRID_W
    n_rb = seq // tq
    n_ck = seq // ck
    assert seq % tq == 0 and n_rb >= 2
    n_chunks = NA_K_ROWS // NA_K_CHUNK

    def chunk0(rb):
        return jnp.clip(2 * rb - 1, 0, n_ck - n_chunks)

    def k_spec(j):
        return pl.BlockSpec((ck, HEAD_W), lambda h, rb, bi: (bi * n_ck + chunk0(rb) + j, k_col + h))

    def v_spec(j):
        return pl.BlockSpec((1, 1, 1, HEAD_W, ck), lambda h, rb, bi: (bi, h, chunk0(rb) + j, 0, 0))

    def variant(rb):
        return jnp.where(rb == 0, 0, jnp.where(rb == n_rb - 1, 2, 1))

    in_specs = ([pl.BlockSpec((tq, HEAD_W), lambda h, rb, bi: (bi * n_rb + rb, q_col + h))]
                + [k_spec(j) for j in range(n_chunks)]
                + [v_spec(j) for j in range(n_chunks)]
                + [pl.BlockSpec((ctx_len, HEAD_W), lambda h, rb, bi: (bi, k_col + h)),
                   pl.BlockSpec((1, 1, HEAD_W, ctx_len), lambda h, rb, bi: (bi, h, 0, 0)),
                   pl.BlockSpec((1, 1, NA_K_ROWS * GRID_W, tq), lambda h, rb, bi: (variant(rb), h, 0, 0))])
    return pl.pallas_call(
        _na_kernel,
        grid=(NA_HEADS, n_rb, b),
        in_specs=in_specs,
        out_specs=pl.BlockSpec((tq, HEAD_W), lambda h, rb, bi: (bi * n_rb + rb, h)),
        out_shape=jax.ShapeDtypeStruct((b * seq, NA_HEADS * HEAD_W), BF16),
        compiler_params=_params(("parallel", "parallel", "parallel")),
    )(*([proj] * (1 + n_chunks) + [vt] * n_chunks + [cproj, vct, bias]))


def _dense_attn_kernel(q_ref, k_ref, v_ref, o_ref):
    s = lax.dot_general(q_ref[...], k_ref[...], _NT, preferred_element_type=F32) * (HEAD_W ** -0.5)
    p = jnp.exp(s - s.max(axis=-1, keepdims=True))
    l = p.sum(axis=-1, keepdims=True)
    o = jnp.dot(p.astype(BF16), v_ref[...], preferred_element_type=F32)
    o_ref[...] = (o / l).astype(o_ref.dtype)


def _dense_attn(cproj, b, ctx_len, q_col, k_col, v_col):
    def spec(col):
        return pl.BlockSpec((ctx_len, HEAD_W), lambda bi, h: (bi, col + h))
    return pl.pallas_call(
        _dense_attn_kernel,
        grid=(b, NA_HEADS),
        in_specs=[spec(q_col), spec(k_col), spec(v_col)],
        out_specs=pl.BlockSpec((ctx_len, HEAD_W), lambda bi, h: (bi, h)),
        out_shape=jax.ShapeDtypeStruct((b * ctx_len, NA_HEADS * HEAD_W), BF16),
        compiler_params=_params(("parallel", "parallel")),
    )(cproj, cproj, cproj)


def _conformer_kernel(a_ref, g_ref, pa_ref, pg_ref, na_ref, ng_ref, w_ref, b_ref, lg_ref, lb_ref,
                      o_ref, ext_ref, sh_ref, *, tps, chunk):
    tm = a_ref.shape[0]
    kw = w_ref.shape[0]
    t = pl.program_id(0) % tps

    def glu(a, g):
        return a[...].astype(F32) * _sigmoid(g[...].astype(F32))

    ext_ref[HALO:HALO + tm, :] = glu(a_ref, g_ref)
    ext_ref[0:HALO, :] = jnp.where(t == 0, 0.0, glu(pa_ref, pg_ref))
    ext_ref[HALO + tm:, :] = jnp.where(t == tps - 1, 0.0, glu(na_ref, ng_ref))
    n_sh = sh_ref.shape[1]
    for ph in range(SUBLANES):
        sh_ref[ph] = ext_ref[ph:ph + n_sh, :]
    off = HALO - kw // 2
    for r in range(tm // chunk):
        acc = jnp.zeros((chunk, a_ref.shape[1]), F32) + b_ref[...]
        for k in range(kw):
            blk, ph = divmod(off + k, SUBLANES)
            row0 = r * chunk + blk * SUBLANES
            acc = acc + sh_ref[ph, row0:row0 + chunk, :] * w_ref[k:k + 1, :]
        mu = jnp.mean(acc, axis=-1, keepdims=True)
        xc = acc - mu
        y = xc * lax.rsqrt(jnp.mean(xc * xc, axis=-1, keepdims=True) + EPS) * lg_ref[...] + lb_ref[...]
        o_ref[r * chunk:(r + 1) * chunk, :] = (y * _sigmoid(y)).astype(o_ref.dtype)


def _conformer(proj, conv_w, conv_b, ln_g, ln_b, seq, tm, a_col, g_col):
    n = proj.shape[0]
    kw, ch = conv_w.shape
    assert kw // 2 < HALO and seq % tm == 0 and tm % HALO == 0
    tps = seq // tm
    r = tm // HALO
    last = n // HALO - 1

    def main(col):
        return pl.BlockSpec((tm, ch), lambda i: (i, col))

    def prev(col):
        return pl.BlockSpec((HALO, ch), lambda i: (jnp.maximum(i * r - 1, 0), col))

    def nxt(col):
        return pl.BlockSpec((HALO, ch), lambda i: (jnp.minimum((i + 1) * r, last), col))

    vec = pl.BlockSpec((1, ch), lambda i: (0, 0))
    return pl.pallas_call(
        functools.partial(_conformer_kernel, tps=tps, chunk=32),
        grid=(n // tm,),
        in_specs=[main(a_col), main(g_col), prev(a_col), prev(g_col), nxt(a_col), nxt(g_col),
                  pl.BlockSpec((kw, ch), lambda i: (0, 0)), vec, vec, vec],
        out_specs=pl.BlockSpec((tm, ch), lambda i: (i, 0)),
        out_shape=jax.ShapeDtypeStruct((n, ch), BF16),
        scratch_shapes=[pltpu.VMEM((tm + 2 * HALO, ch), F32),
                        pltpu.VMEM((SUBLANES, tm + 2 * HALO - SUBLANES, ch), F32)],
        compiler_params=_params(("parallel",)),
    )(proj, proj, proj, proj, proj, proj, conv_w, conv_b.reshape(1, ch), ln_g.reshape(1, ch),
      ln_b.reshape(1, ch))


def _diff_kernel(q_ref, k_ref, vt_ref, lq1_ref, lk1_ref, lq2_ref, lk2_ref, g_ref, o_ref,
                 s_ref, p_ref, m_ref, l_ref, al_ref, acc_ref, *, lam_init):
    tq = q_ref.shape[0]
    nkb = k_ref.shape[1]
    q = q_ref[...]
    lane = lax.broadcasted_iota(jnp.int32, q.shape, 1)
    zero = jnp.zeros_like(q)
    qbd = jnp.concatenate([jnp.where(lane < DIFF_QK_DIM, q, zero),
                           jnp.where(lane >= DIFF_QK_DIM, q, zero)], axis=0)
    m_ref[...] = jnp.full(m_ref.shape, NEG, F32)
    l_ref[...] = jnp.zeros(l_ref.shape, F32)
    acc_ref[...] = jnp.zeros(acc_ref.shape, F32)

    def scores(kb):
        s_ref[...] = lax.dot_general(k_ref[0, kb], qbd, _NT, preferred_element_type=F32)

    def softmax():
        for c in range(2 * tq // HEAD_W):
            cols = slice(c * HEAD_W, (c + 1) * HEAD_W)
            s = s_ref[:, cols]
            m_old = m_ref[:, cols]
            m_new = jnp.maximum(m_old, s.max(axis=0, keepdims=True))
            alpha = jnp.exp2(m_old - m_new)
            p = jnp.exp2(s - m_new)
            l_ref[:, cols] = alpha * l_ref[:, cols] + p.sum(axis=0, keepdims=True)
            m_ref[:, cols] = m_new
            al_ref[:, cols] = alpha
            p_ref[:, cols] = p.astype(BF16)

    def accumulate(kb):
        acc_ref[...] = al_ref[...] * acc_ref[...] + jnp.dot(vt_ref[0, 0, kb], p_ref[...],
                                                             preferred_element_type=F32)

    def tick(t, carry):
        accumulate(t - 2)
        softmax()
        scores(t)
        return carry

    scores(0)
    if nkb >= 2:
        softmax()
        scores(1)
        lax.fori_loop(2, nkb, tick, 0)
        accumulate(nkb - 2)
    softmax()
    accumulate(nkb - 1)

    lam = (jnp.exp(jnp.sum(lq1_ref[...] * lk1_ref[...], axis=-1, keepdims=True))
           - jnp.exp(jnp.sum(lq2_ref[...] * lk2_ref[...], axis=-1, keepdims=True)) + lam_init)
    o = acc_ref[...] / l_ref[...]
    ot = o[:, :tq] - lam * o[:, tq:]
    y = ot * lax.rsqrt(jnp.mean(ot * ot, axis=0, keepdims=True) + EPS) * g_ref[...]
    o_ref[...] = (y * (1.0 - lam_init)).T.astype(o_ref.dtype)


def _diff_attn(q_arr, q_col, k_all, vt_all, lams, ln_g, lam_init, b, seq, tq):
    nq = seq // tq
    nkb, tk = k_all.shape[1], k_all.shape[2]
    vec = pl.BlockSpec((1, DIFF_QK_DIM), lambda bi, h, qi: (0, 0))
    return pl.pallas_call(
        functools.partial(_diff_kernel, lam_init=lam_init),
        grid=(b, DIFF_HEADS, nq),
        in_specs=[pl.BlockSpec((tq, HEAD_W), lambda bi, h, qi: (bi * nq + qi, q_col + h)),
                  pl.BlockSpec((1, nkb, tk, HEAD_W), lambda bi, h, qi: (bi, 0, 0, h)),
                  pl.BlockSpec((1, 1, nkb, HEAD_W, tk), lambda bi, h, qi: (bi, h, 0, 0, 0)),
                  vec, vec, vec, vec,
                  pl.BlockSpec((HEAD_W, 1), lambda bi, h, qi: (0, 0))],
        out_specs=pl.BlockSpec((tq, HEAD_W), lambda bi, h, qi: (bi * nq + qi, h)),
        out_shape=jax.ShapeDtypeStruct((b * seq, DIFF_HEADS * HEAD_W), BF16),
        scratch_shapes=[pltpu.VMEM((tk, 2 * tq), F32), pltpu.VMEM((tk, 2 * tq), BF16),
                        pltpu.VMEM((1, 2 * tq), F32), pltpu.VMEM((1, 2 * tq), F32),
                        pltpu.VMEM((1, 2 * tq), F32), pltpu.VMEM((HEAD_W, 2 * tq), F32)],
        compiler_params=_params(("parallel", "parallel", "parallel")),
    )(q_arr, k_all, vt_all, *[v.reshape(1, DIFF_QK_DIM) for v in lams], ln_g.reshape(HEAD_W, 1))


def _kv_layout(k_rows, v_rows, b, tk):
    lk = k_rows.shape[1]
    nkb = lk // tk
    k_all = k_rows.reshape(b, nkb, tk, DIFF_HEADS * HEAD_W)
    vt_all = v_rows.reshape(b, nkb, tk, DIFF_HEADS, HEAD_W).transpose(0, 3, 1, 4, 2)
    return k_all, vt_all


def _merge_kernel(oa_ref, ob_ref, oc_ref, g0_ref, g1_ref, g2_ref, pa_ref, pb_ref, pc_ref, wo_ref,
                  gp_ref, gate_ref, x_ref, o_ref):
    def branch(o_r, g_r, p_r):
        return _sigmoid(g_r[...].astype(F32)) * jnp.dot(o_r[...], p_r[...], preferred_element_type=F32)

    y = branch(oa_ref, g0_ref, pa_ref) + branch(ob_ref, g1_ref, pb_ref) + branch(oc_ref, g2_ref, pc_ref)
    mix = jnp.dot(y.astype(BF16), wo_ref[...], preferred_element_type=F32)
    r = mix * lax.rsqrt(jnp.mean(mix * mix, axis=-1, keepdims=True) + EPS) * gp_ref[...]
    o_ref[...] = x_ref[...] + gate_ref[0] * r


def _resident(shape):
    return pl.BlockSpec(shape, lambda i: (0,) * len(shape), pipeline_mode=pl.Buffered(1))


def _merge_out(oa, ob, oc, proj, gate_col, p_a, p_b, p_c, w_out, g_post, gate, x, seq, tm):
    n, d = x.shape
    w = oa.shape[1]
    tps = seq // tm

    def act():
        return pl.BlockSpec((tm, w), lambda i: (i, 0))

    def gcol(k):
        return pl.BlockSpec((tm, d), lambda i: (i, gate_col + k))

    return pl.pallas_call(
        _merge_kernel,
        grid=(n // tm,),
        in_specs=[act(), act(), act(), gcol(0), gcol(1), gcol(2),
                  _resident((w, d)), _resident((w, d)), _resident((w, d)), _resident((d, d)),
                  pl.BlockSpec((1, d), lambda i: (0, 0)),
                  pl.BlockSpec((1, 1, d), lambda i: (i // tps, 0, 0)),
                  pl.BlockSpec((tm, d), lambda i: (i, 0))],
        out_specs=pl.BlockSpec((tm, d), lambda i: (i, 0)),
        out_shape=jax.ShapeDtypeStruct((n, d), F32),
        compiler_params=_params(("parallel",)),
    )(oa, ob, oc, proj, proj, proj, p_a, p_b, p_c, w_out, g_post.reshape(1, d), gate, x)


def _ffn_gate_kernel(a_ref, b_ref, pa_ref, pb_ref, na_ref, nb_ref, wa_ref, wb_ref, ba_ref, bb_ref,
                     o_ref, *, tps):
    tm = a_ref.shape[0]
    t = pl.program_id(0) % tps
    row = lax.broadcasted_iota(jnp.int32, a_ref.shape, 0)

    def conv(x_ref, p_ref, n_ref, w_ref, bias_ref):
        x = x_ref[...].astype(F32)
        before = jnp.where(t == 0, 0.0, p_ref[HALO - 1:HALO, :].astype(F32))
        after = jnp.where(t == tps - 1, 0.0, n_ref[0:1, :].astype(F32))
        xm = jnp.where(row == 0, before, pltpu.roll(x, 1, axis=0))
        xp = jnp.where(row == tm - 1, after, pltpu.roll(x, tm - 1, axis=0))
        return xm * w_ref[0:1, :] + x * w_ref[1:2, :] + xp * w_ref[2:3, :] + bias_ref[...]

    a = conv(a_ref, pa_ref, na_ref, wa_ref, ba_ref)
    b = conv(b_ref, pb_ref, nb_ref, wb_ref, bb_ref)
    o_ref[...] = (a * _sigmoid(a) * b).astype(o_ref.dtype)


def _ffn_gate(u, conv_w, conv_b, seq, tm, tc):
    n, f2 = u.shape
    f = f2 // 2
    kw = conv_w.shape[0]
    assert kw == 3 and f % tc == 0
    nc = f // tc
    tps = seq // tm
    r = tm // HALO
    last = n // HALO - 1

    def main(o):
        return pl.BlockSpec((tm, tc), lambda i, j: (i, j + o))

    def prev(o):
        return pl.BlockSpec((HALO, tc), lambda i, j: (jnp.maximum(i * r - 1, 0), j + o))

    def nxt(o):
        return pl.BlockSpec((HALO, tc), lambda i, j: (jnp.minimum((i + 1) * r, last), j + o))

    def wspec(o):
        return pl.BlockSpec((kw, tc), lambda i, j: (0, j + o))

    def bspec(o):
        return pl.BlockSpec((1, tc), lambda i, j: (0, j + o))

    cb = conv_b.reshape(1, f2)
    return pl.pallas_call(
        functools.partial(_ffn_gate_kernel, tps=tps),
        grid=(n // tm, nc),
        in_specs=[main(0), main(nc), prev(0), prev(nc), nxt(0), nxt(nc),
                  wspec(0), wspec(nc), bspec(0), bspec(nc)],
        out_specs=pl.BlockSpec((tm, tc), lambda i, j: (i, j)),
        out_shape=jax.ShapeDtypeStruct((n, f), BF16),
        compiler_params=_params(("parallel", "parallel")),
    )(u, u, u, u, u, u, conv_w, conv_w, cb, cb)


def _down_kernel(h_ref, w_ref, gp_ref, gate_ref, x_ref, o_ref):
    y = jnp.dot(h_ref[...], w_ref[...], preferred_element_type=F32)
    r = y * lax.rsqrt(jnp.mean(y * y, axis=-1, keepdims=True) + EPS) * gp_ref[...]
    o_ref[...] = x_ref[...] + gate_ref[0] * r


def _down_norm_res(h, w, g_post, gate, x, seq, tm):
    n, d = x.shape
    f = h.shape[1]
    tps = seq // tm
    return pl.pallas_call(
        _down_kernel,
        grid=(n // tm,),
        in_specs=[pl.BlockSpec((tm, f), lambda i: (i, 0)),
                  _resident((f, d)),
                  pl.BlockSpec((1, d), lambda i: (0, 0)),
                  pl.BlockSpec((1, 1, d), lambda i: (i // tps, 0, 0)),
                  pl.BlockSpec((tm, d), lambda i: (i, 0))],
        out_specs=pl.BlockSpec((tm, d), lambda i: (i, 0)),
        out_shape=jax.ShapeDtypeStruct((n, d), F32),
        compiler_params=_params(("parallel",)),
    )(h, w, g_post.reshape(1, d), gate, x)


def kernel(x, c, ctx, c_ctx, w_mod, b_mod, g_pre_mix, w_in, na_rpb, conv_w, conv_b, conv_ln_g,
           conv_ln_b, lam_q1, lam_k1, lam_q2, lam_k2, diff_ln_g, p_a, p_b, p_c, w_out, g_post_mix,
           g_pre_ffn, w_up, ffn_conv_w, ffn_conv_b, w_down, g_post_ffn):
    b, seq, d = x.shape
    ctx_len = ctx.shape[1]
    depth = w_in.shape[0]
    na_w = NA_HEADS * HEAD_W
    conv_ch = conv_w.shape[2]
    qk_w = DIFF_HEADS * 2 * DIFF_QK_DIM
    u_off = 3 * na_w
    dq_off = u_off + 2 * conv_ch
    dv_off = dq_off + 2 * qk_w
    m_dv = dq_off
    m_gate = m_dv + DIFF_HEADS * HEAD_W
    na_q_col, na_k_col = 0, NA_HEADS
    tn = 1024
    assert m_gate % d == 0 and u_off % conv_ch == 0 and na_w == NA_HEADS * HEAD_W

    xs = x.reshape(b * seq, d)
    cs = ctx.reshape(b * ctx_len, d)
    cond = jnp.concatenate([c, c_ctx[None, :], jnp.zeros((8 - b - 1, d), F32)], axis=0)
    mods = _modulation(cond, w_mod, b_mod)
    rope_tabs = _rope_tables(seq)
    qscale = DIFF_QK_DIM ** -0.5 * math.log2(math.e)
    col_scale = jnp.concatenate([jnp.full((qk_w,), qscale, F32), jnp.ones((qk_w,), F32)])
    tk = 256
    tm_x, tm_c = 1024, ctx_len
    n_ck = seq // (NA_K_CHUNK * GRID_W)

    for l in range(depth):
        last = l == depth - 1
        lam_init = 0.8 - 0.6 * math.exp(-0.3 * l)
        mx = mods[l, :b].reshape(b, 6, 1, d)
        mc = jnp.broadcast_to(mods[l, b].reshape(1, 6, 1, d), (b, 6, 1, d))
        w_main = jnp.concatenate([w_in[l][:, :dq_off], w_in[l][:, dv_off:]], axis=1).astype(BF16)
        w_qk = w_in[l][:, dq_off:dv_off].astype(BF16)
        lams = (lam_q1[l], lam_k1[l], lam_q2[l], lam_k2[l])
        pa_l, pb_l, pc_l, wo_l = (p_a[l].astype(BF16), p_b[l].astype(BF16), p_c[l].astype(BF16),
                                  w_out[l].astype(BF16))
        w_up_l, w_down_l = w_up[l].astype(BF16), w_down[l].astype(BF16)

        proj = _norm_mod_matmul(xs, g_pre_mix[l], mx[:, 0], mx[:, 1], w_main, seq, tm_x, tn)
        qk = _norm_mod_matmul(xs, g_pre_mix[l], mx[:, 0], mx[:, 1], w_qk, seq, tm_x, tn,
                              col_scale=col_scale, rope_tabs=rope_tabs)
        cproj = _norm_mod_matmul(cs, g_pre_mix[l], mc[:, 0], mc[:, 1], w_main, ctx_len, tm_c, tn)
        cqk = _norm_mod_matmul(cs, g_pre_mix[l], mc[:, 0], mc[:, 1], w_qk, ctx_len, tm_c, tn,
                               col_scale=col_scale)
        bias = _na_bias_table(na_rpb[l])
        nv_t = (proj[:, 2 * na_w:3 * na_w].reshape(b, n_ck, NA_K_CHUNK * GRID_W, NA_HEADS, HEAD_W)
                .transpose(0, 3, 1, 4, 2))
        cnv_t = cproj[:, 2 * na_w:3 * na_w].reshape(b, ctx_len, NA_HEADS, HEAD_W).transpose(0, 2, 3, 1)
        oa = _na_latent(proj, cproj, nv_t, cnv_t, bias, b, seq, ctx_len, na_q_col, na_k_col)
        ob = _conformer(proj, conv_w[l], conv_b[l], conv_ln_g[l], conv_ln_b[l], seq, 256,
                        u_off // conv_ch, u_off // conv_ch + 1)
        ck = cqk[:, qk_w:].reshape(b, ctx_len, qk_w)
        cv = cproj[:, m_dv:m_gate].reshape(b, ctx_len, qk_w)
        k_rows = jnp.concatenate([qk[:, qk_w:].reshape(b, seq, qk_w), ck], axis=1)
        v_rows = jnp.concatenate([proj[:, m_dv:m_gate].reshape(b, seq, qk_w), cv], axis=1)
        k_all, vt_all = _kv_layout(k_rows, v_rows, b, tk)
        oc = _diff_attn(qk, 0, k_all, vt_all, lams, diff_ln_g[l], lam_init, b, seq, 512)
        if not last:
            oa_c = _dense_attn(cproj, b, ctx_len, na_q_col, na_k_col, 2 * NA_HEADS)
            ob_c = _conformer(cproj, conv_w[l], conv_b[l], conv_ln_g[l], conv_ln_b[l], ctx_len,
                              ctx_len, u_off // conv_ch, u_off // conv_ch + 1)
            kc_all, vtc_all = _kv_layout(ck, cv, b, tk)
            oc_c = _diff_attn(cqk, 0, kc_all, vtc_all, lams, diff_ln_g[l], lam_init, b, ctx_len, 256)
            cs = _merge_out(oa_c, ob_c, oc_c, cproj, m_gate // d, pa_l, pb_l, pc_l, wo_l,
                            g_post_mix[l], mc[:, 2], cs, ctx_len, 256)
        xs = _merge_out(oa, ob, oc, proj, m_gate // d, pa_l, pb_l, pc_l, wo_l, g_post_mix[l],
                        mx[:, 2], xs, seq, 256)

        u = _norm_mod_matmul(xs, g_pre_ffn[l], mx[:, 3], mx[:, 4], w_up_l, seq, tm_x, tn)
        hg = _ffn_gate(u, ffn_conv_w[l], ffn_conv_b[l], seq, 512, 512)
        xs = _down_norm_res(hg, w_down_l, g_post_ffn[l], mx[:, 5], xs, seq, 256)
        if not last:
            u = _norm_mod_matmul(cs, g_pre_ffn[l], mc[:, 3], mc[:, 4], w_up_l, ctx_len, tm_c, tn)
            hg = _ffn_gate(u, ffn_conv_w[l], ffn_conv_b[l], ctx_len, ctx_len, 512)
            cs = _down_norm_res(hg, w_down_l, g_post_ffn[l], mc[:, 5], cs, ctx_len, 256)
    return xs.reshape(b, seq, d)
```

```python
import functools
import math

import numpy as np
import jax
import jax.numpy as jnp
from jax import lax
from jax.experimental import pallas as pl
from jax.experimental.pallas import tpu as pltpu

F32 = jnp.float32
BF16 = jnp.bfloat16

EPS = 1e-6
GRID_W = 64
NA_HEADS = 8
NA_WIN_ROWS = 8
NA_WIN_COLS = 16
DIFF_HEADS = 8
DIFF_QK_DIM = 64
HEAD_W = 128
SUBLANES = 8
ROPE_THETA = 10000.0
NEG = -1e30

NA_Q_ROWS = 8
NA_K_ROWS = 16
NA_K_CHUNK = 4
HALO = 16

NMM_M_SPLIT = 512
DIFF_TK = 1408

VMEM_LIMIT = 56 * 1024 * 1024
_NT = (((1,), (1,)), ((), ()))


def _params(sem):
    return pltpu.CompilerParams(dimension_semantics=sem, vmem_limit_bytes=VMEM_LIMIT)


def _sigmoid(x):
    return 1.0 / (1.0 + jnp.exp(-x))


def _mod_kernel(c_ref, w_ref, b_ref, o_ref):
    c = c_ref[...]
    h = (c * _sigmoid(c)).astype(BF16)
    o_ref[0] = jnp.dot(h, w_ref[0].astype(BF16), preferred_element_type=F32) + b_ref[0]


def _modulation(cond, w_mod, b_mod):
    nl, d, n = w_mod.shape
    tn = 1024
    return pl.pallas_call(
        _mod_kernel,
        grid=(nl, n // tn),
        in_specs=[pl.BlockSpec((8, d), lambda l, j: (0, 0)),
                  pl.BlockSpec((1, d, tn), lambda l, j: (l, 0, j)),
                  pl.BlockSpec((1, 1, tn), lambda l, j: (l, 0, j))],
        out_specs=pl.BlockSpec((1, 8, tn), lambda l, j: (l, 0, j)),
        out_shape=jax.ShapeDtypeStruct((nl, 8, n), F32),
        compiler_params=_params(("parallel", "parallel")),
    )(cond, w_mod, b_mod.reshape(nl, 1, n))


def _rope_block(blk, cos, s1, s2):
    return blk * cos + pltpu.roll(blk, HEAD_W - 16, axis=1) * s1 + pltpu.roll(blk, 16, axis=1) * s2


def _nmm_kernel(*refs, rope, scaled, m_split):
    x_ref, g_ref, sh_ref, sc_ref, w_ref = refs[:5]
    o_ref, h_ref = refs[-2:]
    extra = list(refs[5:-2])
    cs_ref = extra.pop(0) if scaled else None
    cos_ref, s1_ref, s2_ref = extra if rope else (None, None, None)

    @pl.when(pl.program_id(1) == 0)
    def _():
        x = x_ref[...]
        y = x * lax.rsqrt(jnp.mean(x * x, axis=-1, keepdims=True) + EPS) * g_ref[...]
        h_ref[...] = (y * (1.0 + sc_ref[0]) + sh_ref[0]).astype(BF16)

    tm, tn = o_ref.shape
    for mi in range(tm // m_split):
        rows = slice(mi * m_split, (mi + 1) * m_split)
        acc = jnp.dot(h_ref[rows, :], w_ref[...], preferred_element_type=F32)
        if not (rope or scaled):
            o_ref[rows, :] = acc.astype(o_ref.dtype)
            continue
        for hb in range(tn // HEAD_W):
            cols = slice(hb * HEAD_W, (hb + 1) * HEAD_W)
            blk = acc[:, cols]
            if rope:
                blk = _rope_block(blk, cos_ref[rows, :], s1_ref[rows, :], s2_ref[rows, :])
            if scaled:
                blk = blk * cs_ref[:, cols]
            o_ref[rows, cols] = blk.astype(o_ref.dtype)


def _norm_mod_matmul(x, g, shift, scale, w, seq, tm, tn, *, col_scale=None, rope_tabs=None):
    n, d = x.shape
    nout = w.shape[1]
    tps = seq // tm
    rope = rope_tabs is not None
    scaled = col_scale is not None
    in_specs = [pl.BlockSpec((tm, d), lambda i, j: (i, 0)),
                pl.BlockSpec((1, d), lambda i, j: (0, 0)),
                pl.BlockSpec((1, 1, d), lambda i, j: (i // tps, 0, 0)),
                pl.BlockSpec((1, 1, d), lambda i, j: (i // tps, 0, 0)),
                pl.BlockSpec((d, tn), lambda i, j: (0, j))]
    args = [x, g.reshape(1, d), shift, scale, w]
    if scaled:
        in_specs.append(pl.BlockSpec((1, tn), lambda i, j: (0, j)))
        args.append(col_scale.reshape(1, nout))
    if rope:
        in_specs += [pl.BlockSpec((tm, HEAD_W), lambda i, j: (i % tps, 0))] * 3
        args += list(rope_tabs)
    return pl.pallas_call(
        functools.partial(_nmm_kernel, rope=rope, scaled=scaled, m_split=min(tm, NMM_M_SPLIT)),
        grid=(n // tm, nout // tn),
        in_specs=in_specs,
        out_specs=pl.BlockSpec((tm, tn), lambda i, j: (i, j)),
        out_shape=jax.ShapeDtypeStruct((n, nout), BF16),
        scratch_shapes=[pltpu.VMEM((tm, d), BF16)],
        compiler_params=_params(("parallel", "arbitrary")),
    )(*args)


def _rope_tables(seq):
    t = jnp.arange(seq)
    rows, cols = (t // GRID_W).astype(F32), (t % GRID_W).astype(F32)
    half = DIFF_QK_DIM // 2
    inv = ROPE_THETA ** (-jnp.arange(0, half, 2, dtype=F32) / half)
    lane = np.arange(HEAD_W) % DIFF_QK_DIM
    use_col = lane >= half
    e = lane % half
    freq = e % (half // 2)
    second = e >= half // 2
    pos = jnp.where(jnp.asarray(use_col)[None, :], cols[:, None], rows[:, None])
    ang = pos * inv[freq][None, :]
    cos, sin = jnp.cos(ang), jnp.sin(ang)
    s1 = jnp.where(jnp.asarray(second)[None, :], 0.0, -sin)
    s2 = jnp.where(jnp.asarray(second)[None, :], sin, 0.0)
    return cos, s1, s2


def _na_kernel(q_ref, k0, k1, k2, k3, v0, v1, v2, v3, kc_ref, vc_ref, bias_ref, o_ref):
    q = q_ref[...]
    c = HEAD_W ** -0.5 * math.log2(math.e)
    ck = NA_K_CHUNK * GRID_W
    s = [lax.dot_general(kr[...], q, _NT, preferred_element_type=F32) * c
         + bias_ref[0, 0, j * ck:(j + 1) * ck, :] for j, kr in enumerate((k0, k1, k2, k3))]
    s.append(lax.dot_general(kc_ref[...], q, _NT, preferred_element_type=F32) * c)
    m = s[0].max(axis=0, keepdims=True)
    for t in s[1:]:
        m = jnp.maximum(m, t.max(axis=0, keepdims=True))
    p = [jnp.exp2(t - m) for t in s]
    l = p[0].sum(axis=0, keepdims=True)
    for t in p[1:]:
        l = l + t.sum(axis=0, keepdims=True)
    ot = jnp.dot(vc_ref[0, 0], p[4].astype(BF16), preferred_element_type=F32)
    for t, vr in zip(p[:4], (v0, v1, v2, v3)):
        ot = ot + jnp.dot(vr[0, 0, 0], t.astype(BF16), preferred_element_type=F32)
    o_ref[...] = (ot / l).T.astype(o_ref.dtype)


def _na_bias_table(rpb):
    i = np.arange(NA_Q_ROWS)[:, None]
    jr = np.arange(NA_K_ROWS)[None, :]
    half = NA_WIN_ROWS // 2
    lo_top = np.maximum(i - half, 0)
    lo_int = i + 0 * jr
    lo_bot = NA_Q_ROWS + np.minimum(i - half, 0)
    offs = (0, -half, -NA_Q_ROWS)
    dr, rv = [], []
    for off, lo in zip(offs, (lo_top, lo_int, lo_bot)):
        dr.append(off + jr - i + NA_WIN_ROWS - 1)
        rv.append((jr >= lo) & (jr < lo + NA_WIN_ROWS))
    dr, rv = np.stack(dr), np.stack(rv)
    qc = np.arange(GRID_W)[:, None]
    kc = np.arange(GRID_W)[None, :]
    c0 = np.clip(qc - NA_WIN_COLS // 2, 0, GRID_W - NA_WIN_COLS)
    dc = kc - qc + NA_WIN_COLS - 1
    cv = (kc >= c0) & (kc < c0 + NA_WIN_COLS)
    drc = np.clip(dr, 0, 2 * NA_WIN_ROWS - 2)
    dcc = np.clip(dc, 0, 2 * NA_WIN_COLS - 2)
    t = rpb.astype(F32)[:, drc]
    t = t[..., dcc]
    valid = rv[:, :, :, None, None] & cv[None, None, None]
    t = jnp.where(jnp.asarray(valid)[None], t * math.log2(math.e), NEG)
    t = t.transpose(1, 0, 3, 5, 2, 4)
    h = rpb.shape[0]
    return t.reshape(3, h, NA_K_ROWS * GRID_W, NA_Q_ROWS * GRID_W)


def _na_latent(proj, cproj, vt, vct, bias, b, seq, ctx_len, q_col, k_col):
    tq = NA_Q_ROWS * GRID_W
    ck = NA_K_CHUNK * GRID_W
    n_rb = seq // tq
    n_ck = seq // ck
    assert seq % tq == 0 and n_rb >= 2
    n_chunks = NA_K_ROWS // NA_K_CHUNK

    def chunk0(rb):
        return jnp.clip(2 * rb - 1, 0, n_ck - n_chunks)

    def k_spec(j):
        return pl.BlockSpec((ck, HEAD_W), lambda h, rb, bi: (bi * n_ck + chunk0(rb) + j, k_col + h))

    def v_spec(j):
        return pl.BlockSpec((1, 1, 1, HEAD_W, ck), lambda h, rb, bi: (bi, h, chunk0(rb) + j, 0, 0))

    def variant(rb):
        return jnp.where(rb == 0, 0, jnp.where(rb == n_rb - 1, 2, 1))

    in_specs = ([pl.BlockSpec((tq, HEAD_W), lambda h, rb, bi: (bi * n_rb + rb, q_col + h))]
                + [k_spec(j) for j in range(n_chunks)]
                + [v_spec(j) for j in range(n_chunks)]
                + [pl.BlockSpec((ctx_len, HEAD_W), lambda h, rb, bi: (bi, k_col + h)),
                   pl.BlockSpec((1, 1, HEAD_W, ctx_len), lambda h, rb, bi: (bi, h, 0, 0)),
                   pl.BlockSpec((1, 1, NA_K_ROWS * GRID_W, tq), lambda h, rb, bi: (variant(rb), h, 0, 0))])
    return pl.pallas_call(
        _na_kernel,
        grid=(NA_HEADS, n_rb, b),
        in_specs=in_specs,
        out_specs=pl.BlockSpec((tq, HEAD_W), lambda h, rb, bi: (bi * n_rb + rb, h)),
        out_shape=jax.ShapeDtypeStruct((b * seq, NA_HEADS * HEAD_W), BF16),
        compiler_params=_params(("parallel", "parallel", "parallel")),
    )(*([proj] * (1 + n_chunks) + [vt] * n_chunks + [cproj, vct, bias]))


def _dense_attn_kernel(q_ref, k_ref, v_ref, o_ref):
    s = lax.dot_general(q_ref[...], k_ref[...], _NT, preferred_element_type=F32) * (HEAD_W ** -0.5)
    p = jnp.exp(s - s.max(axis=-1, keepdims=True))
    l = p.sum(axis=-1, keepdims=True)
    o = jnp.dot(p.astype(BF16), v_ref[...], preferred_element_type=F32)
    o_ref[...] = (o / l).astype(o_ref.dtype)


def _dense_attn(cproj, b, ctx_len, q_col, k_col, v_col):
    def spec(col):
        return pl.BlockSpec((ctx_len, HEAD_W), lambda bi, h: (bi, col + h))
    return pl.pallas_call(
        _dense_attn_kernel,
        grid=(b, NA_HEADS),
        in_specs=[spec(q_col), spec(k_col), spec(v_col)],
        out_specs=pl.BlockSpec((ctx_len, HEAD_W), lambda bi, h: (bi, h)),
        out_shape=jax.ShapeDtypeStruct((b * ctx_len, NA_HEADS * HEAD_W), BF16),
        compiler_params=_params(("parallel", "parallel")),
    )(cproj, cproj, cproj)


def _conformer_kernel(a_ref, g_ref, pa_ref, pg_ref, na_ref, ng_ref, w_ref, b_ref, lg_ref, lb_ref,
                      o_ref, ext_ref, sh_ref, *, tps, chunk):
    tm = a_ref.shape[0]
    kw = w_ref.shape[0]
    t = pl.program_id(0) % tps

    def glu(a, g):
        return a[...].astype(F32) * _sigmoid(g[...].astype(F32))

    ext_ref[HALO:HALO + tm, :] = glu(a_ref, g_ref)
    ext_ref[0:HALO, :] = jnp.where(t == 0, 0.0, glu(pa_ref, pg_ref))
    ext_ref[HALO + tm:, :] = jnp.where(t == tps - 1, 0.0, glu(na_ref, ng_ref))
    n_sh = sh_ref.shape[1]
    for ph in range(SUBLANES):
        sh_ref[ph] = ext_ref[ph:ph + n_sh, :]
    off = HALO - kw // 2
    for r in range(tm // chunk):
        acc = jnp.zeros((chunk, a_ref.shape[1]), F32) + b_ref[...]
        for k in range(kw):
            blk, ph = divmod(off + k, SUBLANES)
            row0 = r * chunk + blk * SUBLANES
            acc = acc + sh_ref[ph, row0:row0 + chunk, :] * w_ref[k:k + 1, :]
        mu = jnp.mean(acc, axis=-1, keepdims=True)
        xc = acc - mu
        y = xc * lax.rsqrt(jnp.mean(xc * xc, axis=-1, keepdims=True) + EPS) * lg_ref[...] + lb_ref[...]
        o_ref[r * chunk:(r + 1) * chunk, :] = (y * _sigmoid(y)).astype(o_ref.dtype)


def _conformer(proj, conv_w, conv_b, ln_g, ln_b, seq, tm, a_col, g_col):
    n = proj.shape[0]
    kw, ch = conv_w.shape
    assert kw // 2 < HALO and seq % tm == 0 and tm % HALO == 0
    tps = seq // tm
    r = tm // HALO
    last = n // HALO - 1

    def main(col):
        return pl.BlockSpec((tm, ch), lambda i: (i, col))

    def prev(col):
        return pl.BlockSpec((HALO, ch), lambda i: (jnp.maximum(i * r - 1, 0), col))

    def nxt(col):
        return pl.BlockSpec((HALO, ch), lambda i: (jnp.minimum((i + 1) * r, last), col))

    vec = pl.BlockSpec((1, ch), lambda i: (0, 0))
    return pl.pallas_call(
        functools.partial(_conformer_kernel, tps=tps, chunk=32),
        grid=(n // tm,),
        in_specs=[main(a_col), main(g_col), prev(a_col), prev(g_col), nxt(a_col), nxt(g_col),
                  pl.BlockSpec((kw, ch), lambda i: (0, 0)), vec, vec, vec],
        out_specs=pl.BlockSpec((tm, ch), lambda i: (i, 0)),
        out_shape=jax.ShapeDtypeStruct((n, ch), BF16),
        scratch_shapes=[pltpu.VMEM((tm + 2 * HALO, ch), F32),
                        pltpu.VMEM((SUBLANES, tm + 2 * HALO - SUBLANES, ch), F32)],
        compiler_params=_params(("parallel",)),
    )(proj, proj, proj, proj, proj, proj, conv_w, conv_b.reshape(1, ch), ln_g.reshape(1, ch),
      ln_b.reshape(1, ch))


def _diff_kernel(q_ref, k_ref, vt_ref, lq1_ref, lk1_ref, lq2_ref, lk2_ref, g_ref, o_ref,
                 qt_ref, s_ref, p_ref, mb_ref, m_ref, l_ref, al_ref, acc_ref, *, lam_init):
    tq = q_ref.shape[0]
    nkb = k_ref.shape[1]
    qt = q_ref[...].astype(F32).T
    row = lax.broadcasted_iota(jnp.int32, qt.shape, 0)
    qt_ref[:, :tq] = jnp.where(row < DIFF_QK_DIM, qt, 0.0).astype(BF16)
    qt_ref[:, tq:] = jnp.where(row >= DIFF_QK_DIM, qt, 0.0).astype(BF16)
    m_ref[...] = jnp.full(m_ref.shape, NEG, F32)
    l_ref[...] = jnp.zeros(l_ref.shape, F32)
    acc_ref[...] = jnp.zeros(acc_ref.shape, F32)

    def scores(kb):
        s = jnp.dot(k_ref[0, kb], qt_ref[...], preferred_element_type=F32)
        s_ref[...] = s
        mb_ref[...] = s.max(axis=0, keepdims=True)

    def softmax():
        for c in range(2 * tq // HEAD_W):
            cols = slice(c * HEAD_W, (c + 1) * HEAD_W)
            m_old = m_ref[:, cols]
            m_new = jnp.maximum(m_old, mb_ref[:, cols])
            alpha = jnp.exp2(m_old - m_new)
            p = jnp.exp2(s_ref[:, cols] - m_new)
            l_ref[:, cols] = alpha * l_ref[:, cols] + p.sum(axis=0, keepdims=True)
            m_ref[:, cols] = m_new
            al_ref[:, cols] = alpha
            p_ref[:, cols] = p.astype(BF16)

    def accumulate(kb):
        acc_ref[...] = al_ref[...] * acc_ref[...] + jnp.dot(vt_ref[0, 0, kb], p_ref[...],
                                                             preferred_element_type=F32)

    def tick(t, carry):
        accumulate(t - 2)
        softmax()
        scores(t)
        return carry

    scores(0)
    if nkb >= 2:
        softmax()
        scores(1)
        lax.fori_loop(2, nkb, tick, 0)
        accumulate(nkb - 2)
    softmax()
    accumulate(nkb - 1)

    lam = (jnp.exp(jnp.sum(lq1_ref[...] * lk1_ref[...], axis=-1, keepdims=True))
           - jnp.exp(jnp.sum(lq2_ref[...] * lk2_ref[...], axis=-1, keepdims=True)) + lam_init)
    o = acc_ref[...] / l_ref[...]
    ot = o[:, :tq] - lam * o[:, tq:]
    y = ot * lax.rsqrt(jnp.mean(ot * ot, axis=0, keepdims=True) + EPS) * g_ref[...]
    o_ref[...] = (y * (1.0 - lam_init)).T.astype(o_ref.dtype)


def _diff_attn(q_arr, q_col, k_all, vt_all, lams, ln_g, lam_init, b, seq, tq):
    nq = seq // tq
    nkb, tk = k_all.shape[1], k_all.shape[2]
    vec = pl.BlockSpec((1, DIFF_QK_DIM), lambda bi, h, qi: (0, 0))
    return pl.pallas_call(
        functools.partial(_diff_kernel, lam_init=lam_init),
        grid=(b, DIFF_HEADS, nq),
        in_specs=[pl.BlockSpec((tq, HEAD_W), lambda bi, h, qi: (bi * nq + qi, q_col + h)),
                  pl.BlockSpec((1, nkb, tk, HEAD_W), lambda bi, h, qi: (bi, 0, 0, h)),
                  pl.BlockSpec((1, 1, nkb, HEAD_W, tk), lambda bi, h, qi: (bi, h, 0, 0, 0)),
                  vec, vec, vec, vec,
                  pl.BlockSpec((HEAD_W, 1), lambda bi, h, qi: (0, 0))],
        out_specs=pl.BlockSpec((tq, HEAD_W), lambda bi, h, qi: (bi * nq + qi, h)),
        out_shape=jax.ShapeDtypeStruct((b * seq, DIFF_HEADS * HEAD_W), BF16),
        scratch_shapes=[pltpu.VMEM((HEAD_W, 2 * tq), BF16),
                        pltpu.VMEM((tk, 2 * tq), F32), pltpu.VMEM((tk, 2 * tq), BF16),
                        pltpu.VMEM((1, 2 * tq), F32),
                        pltpu.VMEM((1, 2 * tq), F32), pltpu.VMEM((1, 2 * tq), F32),
                        pltpu.VMEM((1, 2 * tq), F32), pltpu.VMEM((HEAD_W, 2 * tq), F32)],
        compiler_params=_params(("parallel", "parallel", "parallel")),
    )(q_arr, k_all, vt_all, *[v.reshape(1, DIFF_QK_DIM) for v in lams], ln_g.reshape(HEAD_W, 1))


def _kv_layout(k_rows, v_rows, b, tk):
    lk = k_rows.shape[1]
    nkb = lk // tk
    k_all = k_rows.reshape(b, nkb, tk, DIFF_HEADS * HEAD_W)
    vt_all = v_rows.reshape(b, nkb, tk, DIFF_HEADS, HEAD_W).transpose(0, 3, 1, 4, 2)
    return k_all, vt_all


def _merge_kernel(oa_ref, ob_ref, oc_ref, g0_ref, g1_ref, g2_ref, pa_ref, pb_ref, pc_ref, wo_ref,
                  gp_ref, gate_ref, x_ref, o_ref):
    def branch(o_r, g_r, p_r):
        return _sigmoid(g_r[...].astype(F32)) * jnp.dot(o_r[...], p_r[...], preferred_element_type=F32)

    y = branch(oa_ref, g0_ref, pa_ref) + branch(ob_ref, g1_ref, pb_ref) + branch(oc_ref, g2_ref, pc_ref)
    mix = jnp.dot(y.astype(BF16), wo_ref[...], preferred_element_type=F32)
    r = mix * lax.rsqrt(jnp.mean(mix * mix, axis=-1, keepdims=True) + EPS) * gp_ref[...]
    o_ref[...] = x_ref[...] + gate_ref[0] * r


def _resident(shape):
    return pl.BlockSpec(shape, lambda i: (0,) * len(shape), pipeline_mode=pl.Buffered(1))


def _merge_out(oa, ob, oc, proj, gate_col, p_a, p_b, p_c, w_out, g_post, gate, x, seq, tm):
    n, d = x.shape
    w = oa.shape[1]
    tps = seq // tm

    def act():
        return pl.BlockSpec((tm, w), lambda i: (i, 0))

    def gcol(k):
        return pl.BlockSpec((tm, d), lambda i: (i, gate_col + k))

    return pl.pallas_call(
        _merge_kernel,
        grid=(n // tm,),
        in_specs=[act(), act(), act(), gcol(0), gcol(1), gcol(2),
                  _resident((w, d)), _resident((w, d)), _resident((w, d)), _resident((d, d)),
                  pl.BlockSpec((1, d), lambda i: (0, 0)),
                  pl.BlockSpec((1, 1, d), lambda i: (i // tps, 0, 0)),
                  pl.BlockSpec((tm, d), lambda i: (i, 0))],
        out_specs=pl.BlockSpec((tm, d), lambda i: (i, 0)),
        out_shape=jax.ShapeDtypeStruct((n, d), F32),
        compiler_params=_params(("parallel",)),
    )(oa, ob, oc, proj, proj, proj, p_a, p_b, p_c, w_out, g_post.reshape(1, d), gate, x)


def _ffn_gate_kernel(a_ref, b_ref, pa_ref, pb_ref, na_ref, nb_ref, wa_ref, wb_ref, ba_ref, bb_ref,
                     o_ref, *, tps):
    tm = a_ref.shape[0]
    t = pl.program_id(0) % tps
    row = lax.broadcasted_iota(jnp.int32, a_ref.shape, 0)

    def conv(x_ref, p_ref, n_ref, w_ref, bias_ref):
        x = x_ref[...].astype(F32)
        before = jnp.where(t == 0, 0.0, p_ref[HALO - 1:HALO, :].astype(F32))
        after = jnp.where(t == tps - 1, 0.0, n_ref[0:1, :].astype(F32))
        xm = jnp.where(row == 0, before, pltpu.roll(x, 1, axis=0))
        xp = jnp.where(row == tm - 1, after, pltpu.roll(x, tm - 1, axis=0))
        return xm * w_ref[0:1, :] + x * w_ref[1:2, :] + xp * w_ref[2:3, :] + bias_ref[...]

    a = conv(a_ref, pa_ref, na_ref, wa_ref, ba_ref)
    b = conv(b_ref, pb_ref, nb_ref, wb_ref, bb_ref)
    o_ref[...] = (a * _sigmoid(a) * b).astype(o_ref.dtype)


def _ffn_gate(u, conv_w, conv_b, seq, tm, tc):
    n, f2 = u.shape
    f = f2 // 2
    kw = conv_w.shape[0]
    assert kw == 3 and f % tc == 0
    nc = f // tc
    tps = seq // tm
    r = tm // HALO
    last = n // HALO - 1

    def main(o):
        return pl.BlockSpec((tm, tc), lambda i, j: (i, j + o))

    def prev(o):
        return pl.BlockSpec((HALO, tc), lambda i, j: (jnp.maximum(i * r - 1, 0), j + o))

    def nxt(o):
        return pl.BlockSpec((HALO, tc), lambda i, j: (jnp.minimum((i + 1) * r, last), j + o))

    def wspec(o):
        return pl.BlockSpec((kw, tc), lambda i, j: (0, j + o))

    def bspec(o):
        return pl.BlockSpec((1, tc), lambda i, j: (0, j + o))

    cb = conv_b.reshape(1, f2)
    return pl.pallas_call(
        functools.partial(_ffn_gate_kernel, tps=tps),
        grid=(n // tm, nc),
        in_specs=[main(0), main(nc), prev(0), prev(nc), nxt(0), nxt(nc),
                  wspec(0), wspec(nc), bspec(0), bspec(nc)],
        out_specs=pl.BlockSpec((tm, tc), lambda i, j: (i, j)),
        out_shape=jax.ShapeDtypeStruct((n, f), BF16),
        compiler_params=_params(("parallel", "parallel")),
    )(u, u, u, u, u, u, conv_w, conv_w, cb, cb)


def _down_kernel(h_ref, w_ref, gp_ref, gate_ref, x_ref, o_ref):
    y = jnp.dot(h_ref[...], w_ref[...], preferred_element_type=F32)
    r = y * lax.rsqrt(jnp.mean(y * y, axis=-1, keepdims=True) + EPS) * gp_ref[...]
    o_ref[...] = x_ref[...] + gate_ref[0] * r


def _down_norm_res(h, w, g_post, gate, x, seq, tm):
    n, d = x.shape
    f = h.shape[1]
    tps = seq // tm
    return pl.pallas_call(
        _down_kernel,
        grid=(n // tm,),
        in_specs=[pl.BlockSpec((tm, f), lambda i: (i, 0)),
                  _resident((f, d)),
                  pl.BlockSpec((1, d), lambda i: (0, 0)),
                  pl.BlockSpec((1, 1, d), lambda i: (i // tps, 0, 0)),
                  pl.BlockSpec((tm, d), lambda i: (i, 0))],
        out_specs=pl.BlockSpec((tm, d), lambda i: (i, 0)),
        out_shape=jax.ShapeDtypeStruct((n, d), F32),
        compiler_params=_params(("parallel",)),
    )(h, w, g_post.reshape(1, d), gate, x)


def kernel(x, c, ctx, c_ctx, w_mod, b_mod, g_pre_mix, w_in, na_rpb, conv_w, conv_b, conv_ln_g,
           conv_ln_b, lam_q1, lam_k1, lam_q2, lam_k2, diff_ln_g, p_a, p_b, p_c, w_out, g_post_mix,
           g_pre_ffn, w_up, ffn_conv_w, ffn_conv_b, w_down, g_post_ffn):
    b, seq, d = x.shape
    ctx_len = ctx.shape[1]
    depth = w_in.shape[0]
    na_w = NA_HEADS * HEAD_W
    conv_ch = conv_w.shape[2]
    qk_w = DIFF_HEADS * 2 * DIFF_QK_DIM
    u_off = 3 * na_w
    dq_off = u_off + 2 * conv_ch
    dv_off = dq_off + 2 * qk_w
    m_dv = dq_off
    m_gate = m_dv + DIFF_HEADS * HEAD_W
    na_q_col, na_k_col = 0, NA_HEADS
    tn = 1024
    assert m_gate % d == 0 and u_off % conv_ch == 0 and na_w == NA_HEADS * HEAD_W

    xs = x.reshape(b * seq, d)
    cs = ctx.reshape(b * ctx_len, d)
    cond = jnp.concatenate([c, c_ctx[None, :], jnp.zeros((8 - b - 1, d), F32)], axis=0)
    mods = _modulation(cond, w_mod, b_mod)
    rope_tabs = _rope_tables(seq)
    qscale = DIFF_QK_DIM ** -0.5 * math.log2(math.e)
    col_scale = jnp.concatenate([jnp.full((qk_w,), qscale, F32), jnp.ones((qk_w,), F32)])
    tk = 256
    tm_x, tm_c = 1024, ctx_len
    n_ck = seq // (NA_K_CHUNK * GRID_W)

    for l in range(depth):
        last = l == depth - 1
        lam_init = 0.8 - 0.6 * math.exp(-0.3 * l)
        mx = mods[l, :b].reshape(b, 6, 1, d)
        mc = jnp.broadcast_to(mods[l, b].reshape(1, 6, 1, d), (b, 6, 1, d))
        w_main = jnp.concatenate([w_in[l][:, :dq_off], w_in[l][:, dv_off:]], axis=1).astype(BF16)
        w_qk = w_in[l][:, dq_off:dv_off].astype(BF16)
        lams = (lam_q1[l], lam_k1[l], lam_q2[l], lam_k2[l])
        pa_l, pb_l, pc_l, wo_l = (p_a[l].astype(BF16), p_b[l].astype(BF16), p_c[l].astype(BF16),
                                  w_out[l].astype(BF16))
        w_up_l, w_down_l = w_up[l].astype(BF16), w_down[l].astype(BF16)

        proj = _norm_mod_matmul(xs, g_pre_mix[l], mx[:, 0], mx[:, 1], w_main, seq, tm_x, tn)
        qk = _norm_mod_matmul(xs, g_pre_mix[l], mx[:, 0], mx[:, 1], w_qk, seq, tm_x, tn,
                              col_scale=col_scale, rope_tabs=rope_tabs)
        cproj = _norm_mod_matmul(cs, g_pre_mix[l], mc[:, 0], mc[:, 1], w_main, ctx_len, tm_c, tn)
        cqk = _norm_mod_matmul(cs, g_pre_mix[l], mc[:, 0], mc[:, 1], w_qk, ctx_len, tm_c, tn,
                               col_scale=col_scale)
        bias = _na_bias_table(na_rpb[l])
        nv_t = (proj[:, 2 * na_w:3 * na_w].reshape(b, n_ck, NA_K_CHUNK * GRID_W, NA_HEADS, HEAD_W)
                .transpose(0, 3, 1, 4, 2))
        cnv_t = cproj[:, 2 * na_w:3 * na_w].reshape(b, ctx_len, NA_HEADS, HEAD_W).transpose(0, 2, 3, 1)
        oa = _na_latent(proj, cproj, nv_t, cnv_t, bias, b, seq, ctx_len, na_q_col, na_k_col)
        ob = _conformer(proj, conv_w[l], conv_b[l], conv_ln_g[l], conv_ln_b[l], seq, 256,
                        u_off // conv_ch, u_off // conv_ch + 1)
        ck = cqk[:, qk_w:].reshape(b, ctx_len, qk_w)
        cv = cproj[:, m_dv:m_gate].reshape(b, ctx_len, qk_w)
        k_rows = jnp.concatenate([qk[:, qk_w:].reshape(b, seq, qk_w), ck], axis=1)
        v_rows = jnp.concatenate([proj[:, m_dv:m_gate].reshape(b, seq, qk_w), cv], axis=1)
        k_all, vt_all = _kv_layout(k_rows, v_rows, b, DIFF_TK)
        oc = _diff_attn(qk, 0, k_all, vt_all, lams, diff_ln_g[l], lam_init, b, seq, 2048)
        if not last:
            oa_c = _dense_attn(cproj, b, ctx_len, na_q_col, na_k_col, 2 * NA_HEADS)
            ob_c = _conformer(cproj, conv_w[l], conv_b[l], conv_ln_g[l], conv_ln_b[l], ctx_len,
                              ctx_len, u_off // conv_ch, u_off // conv_ch + 1)
            kc_all, vtc_all = _kv_layout(ck, cv, b, tk)
            oc_c = _diff_attn(cqk, 0, kc_all, vtc_all, lams, diff_ln_g[l], lam_init, b, ctx_len, 256)
            cs = _merge_out(oa_c, ob_c, oc_c, cproj, m_gate // d, pa_l, pb_l, pc_l, wo_l,
                            g_post_mix[l], mc[:, 2], cs, ctx_len, 256)
        xs = _merge_out(oa, ob, oc, proj, m_gate // d, pa_l, pb_l, pc_l, wo_l, g_post_mix[l],
                        mx[:, 2], xs, seq, 256)

        u = _norm_mod_matmul(xs, g_pre_ffn[l], mx[:, 3], mx[:, 4], w_up_l, seq, tm_x, tn)
        hg = _ffn_gate(u, ffn_conv_w[l], ffn_conv_b[l], seq, 512, 512)
        xs = _down_norm_res(hg, w_down_l, g_post_ffn[l], mx[:, 5], xs, seq, 256)
        if not last:
            u = _norm_mod_matmul(cs, g_pre_ffn[l], mc[:, 3], mc[:, 4], w_up_l, ctx_len, tm_c, tn)
            hg = _ffn_gate(u, ffn_conv_w[l], ffn_conv_b[l], ctx_len, ctx_len, 512)
            cs = _down_norm_res(hg, w_down_l, g_post_ffn[l], mc[:, 5], cs, ctx_len, 256)
    return xs.reshape(b, seq, d)
```

```python
import functools
import math

import numpy as np
import jax
import jax.numpy as jnp
from jax import lax
from jax.experimental import pallas as pl
from jax.experimental.pallas import tpu as pltpu

F32 = jnp.float32
BF16 = jnp.bfloat16

EPS = 1e-6
GRID_W = 64
NA_HEADS = 8
NA_WIN_ROWS = 8
NA_WIN_COLS = 16
DIFF_HEADS = 8
DIFF_QK_DIM = 64
HEAD_W = 128
SUBLANES = 8
ROPE_THETA = 10000.0
NEG = -1e30

NA_Q_ROWS = 8
NA_K_ROWS = 16
NA_K_CHUNK = 4
HALO = 16

NMM_M_SPLIT = 512
DIFF_TK = 1408

VMEM_LIMIT = 56 * 1024 * 1024
_NT = (((1,), (1,)), ((), ()))


def _params(sem):
    return pltpu.CompilerParams(dimension_semantics=sem, vmem_limit_bytes=VMEM_LIMIT)


def _sigmoid(x):
    return 1.0 / (1.0 + jnp.exp(-x))


def _mod_kernel(c_ref, w_ref, b_ref, o_ref):
    c = c_ref[...]
    h = (c * _sigmoid(c)).astype(BF16)
    o_ref[0] = jnp.dot(h, w_ref[0].astype(BF16), preferred_element_type=F32) + b_ref[0]


def _modulation(cond, w_mod, b_mod):
    nl, d, n = w_mod.shape
    tn = 1024
    return pl.pallas_call(
        _mod_kernel,
        grid=(nl, n // tn),
        in_specs=[pl.BlockSpec((8, d), lambda l, j: (0, 0)),
                  pl.BlockSpec((1, d, tn), lambda l, j: (l, 0, j)),
                  pl.BlockSpec((1, 1, tn), lambda l, j: (l, 0, j))],
        out_specs=pl.BlockSpec((1, 8, tn), lambda l, j: (l, 0, j)),
        out_shape=jax.ShapeDtypeStruct((nl, 8, n), F32),
        compiler_params=_params(("parallel", "parallel")),
    )(cond, w_mod, b_mod.reshape(nl, 1, n))


def _rope_block(blk, cos, s1, s2):
    return blk * cos + pltpu.roll(blk, HEAD_W - 16, axis=1) * s1 + pltpu.roll(blk, 16, axis=1) * s2


def _nmm_kernel(*refs, rope, scaled, m_split):
    x_ref, g_ref, sh_ref, sc_ref, w_ref = refs[:5]
    o_ref, h_ref = refs[-2:]
    extra = list(refs[5:-2])
    cs_ref = extra.pop(0) if scaled else None
    cos_ref, s1_ref, s2_ref = extra if rope else (None, None, None)

    @pl.when(pl.program_id(1) == 0)
    def _():
        x = x_ref[...]
        y = x * lax.rsqrt(jnp.mean(x * x, axis=-1, keepdims=True) + EPS) * g_ref[...]
        h_ref[...] = (y * (1.0 + sc_ref[0]) + sh_ref[0]).astype(BF16)

    tm, tn = o_ref.shape
    for mi in range(tm // m_split):
        rows = slice(mi * m_split, (mi + 1) * m_split)
        acc = jnp.dot(h_ref[rows, :], w_ref[...], preferred_element_type=F32)
        if not (rope or scaled):
            o_ref[rows, :] = acc.astype(o_ref.dtype)
            continue
        for hb in range(tn // HEAD_W):
            cols = slice(hb * HEAD_W, (hb + 1) * HEAD_W)
            blk = acc[:, cols]
            if rope:
                blk = _rope_block(blk, cos_ref[rows, :], s1_ref[rows, :], s2_ref[rows, :])
            if scaled:
                blk = blk * cs_ref[:, cols]
            o_ref[rows, cols] = blk.astype(o_ref.dtype)


def _norm_mod_matmul(x, g, shift, scale, w, seq, tm, tn, *, col_scale=None, rope_tabs=None):
    n, d = x.shape
    nout = w.shape[1]
    tps = seq // tm
    rope = rope_tabs is not None
    scaled = col_scale is not None
    in_specs = [pl.BlockSpec((tm, d), lambda i, j: (i, 0)),
                pl.BlockSpec((1, d), lambda i, j: (0, 0)),
                pl.BlockSpec((1, 1, d), lambda i, j: (i // tps, 0, 0)),
                pl.BlockSpec((1, 1, d), lambda i, j: (i // tps, 0, 0)),
                pl.BlockSpec((d, tn), lambda i, j: (0, j))]
    args = [x, g.reshape(1, d), shift, scale, w]
    if scaled:
        in_specs.append(pl.BlockSpec((1, tn), lambda i, j: (0, j)))
        args.append(col_scale.reshape(1, nout))
    if rope:
        in_specs += [pl.BlockSpec((tm, HEAD_W), lambda i, j: (i % tps, 0))] * 3
        args += list(rope_tabs)
    return pl.pallas_call(
        functools.partial(_nmm_kernel, rope=rope, scaled=scaled, m_split=min(tm, NMM_M_SPLIT)),
        grid=(n // tm, nout // tn),
        in_specs=in_specs,
        out_specs=pl.BlockSpec((tm, tn), lambda i, j: (i, j)),
        out_shape=jax.ShapeDtypeStruct((n, nout), BF16),
        scratch_shapes=[pltpu.VMEM((tm, d), BF16)],
        compiler_params=_params(("parallel", "arbitrary")),
    )(*args)


def _rope_tables(seq):
    t = jnp.arange(seq)
    rows, cols = (t // GRID_W).astype(F32), (t % GRID_W).astype(F32)
    half = DIFF_QK_DIM // 2
    inv = ROPE_THETA ** (-jnp.arange(0, half, 2, dtype=F32) / half)
    lane = np.arange(HEAD_W) % DIFF_QK_DIM
    use_col = lane >= half
    e = lane % half
    freq = e % (half // 2)
    second = e >= half // 2
    pos = jnp.where(jnp.asarray(use_col)[None, :], cols[:, None], rows[:, None])
    ang = pos * inv[freq][None, :]
    cos, sin = jnp.cos(ang), jnp.sin(ang)
    s1 = jnp.where(jnp.asarray(second)[None, :], 0.0, -sin)
    s2 = jnp.where(jnp.asarray(second)[None, :], sin, 0.0)
    return cos, s1, s2


def _na_kernel(q_ref, k0, k1, k2, k3, v0, v1, v2, v3, kc_ref, vc_ref, bias_ref, o_ref):
    q = q_ref[...]
    c = HEAD_W ** -0.5 * math.log2(math.e)
    ck = NA_K_CHUNK * GRID_W
    s = [lax.dot_general(kr[...], q, _NT, preferred_element_type=F32) * c
         + bias_ref[0, 0, j * ck:(j + 1) * ck, :] for j, kr in enumerate((k0, k1, k2, k3))]
    s.append(lax.dot_general(kc_ref[...], q, _NT, preferred_element_type=F32) * c)
    m = s[0].max(axis=0, keepdims=True)
    for t in s[1:]:
        m = jnp.maximum(m, t.max(axis=0, keepdims=True))
    p = [jnp.exp2(t - m) for t in s]
    l = p[0].sum(axis=0, keepdims=True)
    for t in p[1:]:
        l = l + t.sum(axis=0, keepdims=True)
    ot = jnp.dot(vc_ref[0, 0], p[4].astype(BF16), preferred_element_type=F32)
    for t, vr in zip(p[:4], (v0, v1, v2, v3)):
        ot = ot + jnp.dot(vr[0, 0, 0], t.astype(BF16), preferred_element_type=F32)
    o_ref[...] = (ot / l).T.astype(o_ref.dtype)


def _na_bias_table(rpb):
    i = np.arange(NA_Q_ROWS)[:, None]
    jr = np.arange(NA_K_ROWS)[None, :]
    half = NA_WIN_ROWS // 2
    lo_top = np.maximum(i - half, 0)
    lo_int = i + 0 * jr
    lo_bot = NA_Q_ROWS + np.minimum(i - half, 0)
    offs = (0, -half, -NA_Q_ROWS)
    dr, rv = [], []
    for off, lo in zip(offs, (lo_top, lo_int, lo_bot)):
        dr.append(off + jr - i + NA_WIN_ROWS - 1)
        rv.append((jr >= lo) & (jr < lo + NA_WIN_ROWS))
    dr, rv = np.stack(dr), np.stack(rv)
    qc = np.arange(GRID_W)[:, None]
    kc = np.arange(GRID_W)[None, :]
    c0 = np.clip(qc - NA_WIN_COLS // 2, 0, GRID_W - NA_WIN_COLS)
    dc = kc - qc + NA_WIN_COLS - 1
    cv = (kc >= c0) & (kc < c0 + NA_WIN_COLS)
    drc = np.clip(dr, 0, 2 * NA_WIN_ROWS - 2)
    dcc = np.clip(dc, 0, 2 * NA_WIN_COLS - 2)
    t = rpb.astype(F32)[:, drc]
    t = t[..., dcc]
    valid = rv[:, :, :, None, None] & cv[None, None, None]
    t = jnp.where(jnp.asarray(valid)[None], t * math.log2(math.e), NEG)
    t = t.transpose(1, 0, 3, 5, 2, 4)
    h = rpb.shape[0]
    return t.reshape(3, h, NA_K_ROWS * GRID_W, NA_Q_ROWS * GRID_W)


def _na_latent(proj, cproj, vt, vct, bias, b, seq, ctx_len, q_col, k_col):
    tq = NA_Q_ROWS * GRID_W
    ck = NA_K_CHUNK * GRID_W
    n_rb = seq // tq
    n_ck = seq // ck
    assert seq % tq == 0 and n_rb >= 2
    n_chunks = NA_K_ROWS // NA_K_CHUNK

    def chunk0(rb):
        return jnp.clip(2 * rb - 1, 0, n_ck - n_chunks)

    def k_spec(j):
        return pl.BlockSpec((ck, HEAD_W), lambda h, rb, bi: (bi * n_ck + chunk0(rb) + j, k_col + h))

    def v_spec(j):
        return pl.BlockSpec((1, 1, 1, HEAD_W, ck), lambda h, rb, bi: (bi, h, chunk0(rb) + j, 0, 0))

    def variant(rb):
        return jnp.where(rb == 0, 0, jnp.where(rb == n_rb - 1, 2, 1))

    in_specs = ([pl.BlockSpec((tq, HEAD_W), lambda h, rb, bi: (bi * n_rb + rb, q_col + h))]
                + [k_spec(j) for j in range(n_chunks)]
                + [v_spec(j) for j in range(n_chunks)]
                + [pl.BlockSpec((ctx_len, HEAD_W), lambda h, rb, bi: (bi, k_col + h)),
                   pl.BlockSpec((1, 1, HEAD_W, ctx_len), lambda h, rb, bi: (bi, h, 0, 0)),
                   pl.BlockSpec((1, 1, NA_K_ROWS * GRID_W, tq), lambda h, rb, bi: (variant(rb), h, 0, 0))])
    return pl.pallas_call(
        _na_kernel,
        grid=(NA_HEADS, n_rb, b),
        in_specs=in_specs,
        out_specs=pl.BlockSpec((tq, HEAD_W), lambda h, rb, bi: (bi * n_rb + rb, h)),
        out_shape=jax.ShapeDtypeStruct((b * seq, NA_HEADS * HEAD_W), BF16),
        compiler_params=_params(("parallel", "parallel", "parallel")),
    )(*([proj] * (1 + n_chunks) + [vt] * n_chunks + [cproj, vct, bias]))


def _dense_attn_kernel(q_ref, k_ref, v_ref, o_ref):
    s = lax.dot_general(q_ref[...], k_ref[...], _NT, preferred_element_type=F32) * (HEAD_W ** -0.5)
    p = jnp.exp(s - s.max(axis=-1, keepdims=True))
    l = p.sum(axis=-1, keepdims=True)
    o = jnp.dot(p.astype(BF16), v_ref[...], preferred_element_type=F32)
    o_ref[...] = (o / l).astype(o_ref.dtype)


def _dense_attn(cproj, b, ctx_len, q_col, k_col, v_col):
    def spec(col):
        return pl.BlockSpec((ctx_len, HEAD_W), lambda bi, h: (bi, col + h))
    return pl.pallas_call(
        _dense_attn_kernel,
        grid=(b, NA_HEADS),
        in_specs=[spec(q_col), spec(k_col), spec(v_col)],
        out_specs=pl.BlockSpec((ctx_len, HEAD_W), lambda bi, h: (bi, h)),
        out_shape=jax.ShapeDtypeStruct((b * ctx_len, NA_HEADS * HEAD_W), BF16),
        compiler_params=_params(("parallel", "parallel")),
    )(cproj, cproj, cproj)


def _conformer_kernel(a_ref, g_ref, pa_ref, pg_ref, na_ref, ng_ref, w_ref, b_ref, lg_ref, lb_ref,
                      o_ref, ext_ref, sh_ref, *, tps, chunk):
    tm = a_ref.shape[0]
    kw = w_ref.shape[0]
    t = pl.program_id(0) % tps

    def glu(a, g):
        return a[...].astype(F32) * _sigmoid(g[...].astype(F32))

    ext_ref[HALO:HALO + tm, :] = glu(a_ref, g_ref)
    ext_ref[0:HALO, :] = jnp.where(t == 0, 0.0, glu(pa_ref, pg_ref))
    ext_ref[HALO + tm:, :] = jnp.where(t == tps - 1, 0.0, glu(na_ref, ng_ref))
    n_sh = sh_ref.shape[1]
    for ph in range(SUBLANES):
        sh_ref[ph] = ext_ref[ph:ph + n_sh, :]
    off = HALO - kw // 2
    for r in range(tm // chunk):
        acc = jnp.zeros((chunk, a_ref.shape[1]), F32) + b_ref[...]
        for k in range(kw):
            blk, ph = divmod(off + k, SUBLANES)
            row0 = r * chunk + blk * SUBLANES
            acc = acc + sh_ref[ph, row0:row0 + chunk, :] * w_ref[k:k + 1, :]
        mu = jnp.mean(acc, axis=-1, keepdims=True)
        xc = acc - mu
        y = xc * lax.rsqrt(jnp.mean(xc * xc, axis=-1, keepdims=True) + EPS) * lg_ref[...] + lb_ref[...]
        o_ref[r * chunk:(r + 1) * chunk, :] = (y * _sigmoid(y)).astype(o_ref.dtype)


def _conformer(proj, conv_w, conv_b, ln_g, ln_b, seq, tm, a_col, g_col):
    n = proj.shape[0]
    kw, ch = conv_w.shape
    assert kw // 2 < HALO and seq % tm == 0 and tm % HALO == 0
    tps = seq // tm
    r = tm // HALO
    last = n // HALO - 1

    def main(col):
        return pl.BlockSpec((tm, ch), lambda i: (i, col))

    def prev(col):
        return pl.BlockSpec((HALO, ch), lambda i: (jnp.maximum(i * r - 1, 0), col))

    def nxt(col):
        return pl.BlockSpec((HALO, ch), lambda i: (jnp.minimum((i + 1) * r, last), col))

    vec = pl.BlockSpec((1, ch), lambda i: (0, 0))
    return pl.pallas_call(
        functools.partial(_conformer_kernel, tps=tps, chunk=32),
        grid=(n // tm,),
        in_specs=[main(a_col), main(g_col), prev(a_col), prev(g_col), nxt(a_col), nxt(g_col),
                  pl.BlockSpec((kw, ch), lambda i: (0, 0)), vec, vec, vec],
        out_specs=pl.BlockSpec((tm, ch), lambda i: (i, 0)),
        out_shape=jax.ShapeDtypeStruct((n, ch), BF16),
        scratch_shapes=[pltpu.VMEM((tm + 2 * HALO, ch), F32),
                        pltpu.VMEM((SUBLANES, tm + 2 * HALO - SUBLANES, ch), F32)],
        compiler_params=_params(("parallel",)),
    )(proj, proj, proj, proj, proj, proj, conv_w, conv_b.reshape(1, ch), ln_g.reshape(1, ch),
      ln_b.reshape(1, ch))


def _diff_kernel(q_ref, k_ref, vt_ref, lq1_ref, lk1_ref, lq2_ref, lk2_ref, g_ref, o_ref,
                 qt_ref, s_ref, p_ref, mb_ref, m_ref, l_ref, al_ref, acc_ref, *, lam_init):
    tq = q_ref.shape[0]
    nkb = k_ref.shape[1]
    qt = q_ref[...].astype(F32).T
    row = lax.broadcasted_iota(jnp.int32, qt.shape, 0)
    qt_ref[:, :tq] = jnp.where(row < DIFF_QK_DIM, qt, 0.0).astype(BF16)
    qt_ref[:, tq:] = jnp.where(row >= DIFF_QK_DIM, qt, 0.0).astype(BF16)
    m_ref[...] = jnp.full(m_ref.shape, NEG, F32)
    l_ref[...] = jnp.zeros(l_ref.shape, F32)
    acc_ref[...] = jnp.zeros(acc_ref.shape, F32)

    def scores(kb):
        s = jnp.dot(k_ref[0, kb], qt_ref[...], preferred_element_type=F32)
        s_ref[...] = s
        mb_ref[...] = s.max(axis=0, keepdims=True)

    def softmax():
        for c in range(2 * tq // HEAD_W):
            cols = slice(c * HEAD_W, (c + 1) * HEAD_W)
            m_old = m_ref[:, cols]
            m_new = jnp.maximum(m_old, mb_ref[:, cols])
            alpha = jnp.exp2(m_old - m_new)
            p = jnp.exp2(s_ref[:, cols] - m_new)
            l_ref[:, cols] = alpha * l_ref[:, cols] + p.sum(axis=0, keepdims=True)
            m_ref[:, cols] = m_new
            al_ref[:, cols] = alpha
            p_ref[:, cols] = p.astype(BF16)

    def accumulate(kb):
        acc_ref[...] = al_ref[...] * acc_ref[...] + jnp.dot(vt_ref[0, 0, kb], p_ref[...],
                                                             preferred_element_type=F32)

    def tick(t, carry):
        accumulate(t - 2)
        softmax()
        scores(t)
        return carry

    scores(0)
    if nkb >= 2:
        softmax()
        scores(1)
        lax.fori_loop(2, nkb, tick, 0)
        accumulate(nkb - 2)
    softmax()
    accumulate(nkb - 1)

    lam = (jnp.exp(jnp.sum(lq1_ref[...] * lk1_ref[...], axis=-1, keepdims=True))
           - jnp.exp(jnp.sum(lq2_ref[...] * lk2_ref[...], axis=-1, keepdims=True)) + lam_init)
    o = acc_ref[...] / l_ref[...]
    ot = o[:, :tq] - lam * o[:, tq:]
    y = ot * lax.rsqrt(jnp.mean(ot * ot, axis=0, keepdims=True) + EPS) * g_ref[...]
    o_ref[...] = (y * (1.0 - lam_init)).T.astype(o_ref.dtype)


def _diff_attn(q_arr, q_col, k_all, vt_all, lams, ln_g, lam_init, b, seq, tq):
    nq = seq // tq
    nkb, tk = k_all.shape[1], k_all.shape[2]
    vec = pl.BlockSpec((1, DIFF_QK_DIM), lambda bi, h, qi: (0, 0))
    return pl.pallas_call(
        functools.partial(_diff_kernel, lam_init=lam_init),
        grid=(b, DIFF_HEADS, nq),
        in_specs=[pl.BlockSpec((tq, HEAD_W), lambda bi, h, qi: (bi * nq + qi, q_col + h)),
                  pl.BlockSpec((1, nkb, tk, HEAD_W), lambda bi, h, qi: (bi, 0, 0, h)),
                  pl.BlockSpec((1, 1, nkb, HEAD_W, tk), lambda bi, h, qi: (bi, h, 0, 0, 0)),
                  vec, vec, vec, vec,
                  pl.BlockSpec((HEAD_W, 1), lambda bi, h, qi: (0, 0))],
        out_specs=pl.BlockSpec((tq, HEAD_W), lambda bi, h, qi: (bi * nq + qi, h)),
        out_shape=jax.ShapeDtypeStruct((b * seq, DIFF_HEADS * HEAD_W), BF16),
        scratch_shapes=[pltpu.VMEM((HEAD_W, 2 * tq), BF16),
                        pltpu.VMEM((tk, 2 * tq), F32), pltpu.VMEM((tk, 2 * tq), BF16),
                        pltpu.VMEM((1, 2 * tq), F32),
                        pltpu.VMEM((1, 2 * tq), F32), pltpu.VMEM((1, 2 * tq), F32),
                        pltpu.VMEM((1, 2 * tq), F32), pltpu.VMEM((HEAD_W, 2 * tq), F32)],
        compiler_params=_params(("parallel", "parallel", "parallel")),
    )(q_arr, k_all, vt_all, *[v.reshape(1, DIFF_QK_DIM) for v in lams], ln_g.reshape(HEAD_W, 1))


def _largest_tile(n, limit):
    return max(t for t in range(HEAD_W, limit + 1, HEAD_W) if n % t == 0)


def _kv_layout(k_rows, v_rows, b, tk):
    lk = k_rows.shape[1]
    nkb = lk // tk
    k_all = k_rows.reshape(b, nkb, tk, DIFF_HEADS * HEAD_W)
    vt_all = v_rows.reshape(b, nkb, tk, DIFF_HEADS, HEAD_W).transpose(0, 3, 1, 4, 2)
    return k_all, vt_all


def _merge_kernel(oa_ref, ob_ref, oc_ref, g0_ref, g1_ref, g2_ref, pa_ref, pb_ref, pc_ref, wo_ref,
                  gp_ref, gate_ref, x_ref, o_ref):
    def branch(o_r, g_r, p_r):
        return _sigmoid(g_r[...].astype(F32)) * jnp.dot(o_r[...], p_r[...], preferred_element_type=F32)

    y = branch(oa_ref, g0_ref, pa_ref) + branch(ob_ref, g1_ref, pb_ref) + branch(oc_ref, g2_ref, pc_ref)
    mix = jnp.dot(y.astype(BF16), wo_ref[...], preferred_element_type=F32)
    r = mix * lax.rsqrt(jnp.mean(mix * mix, axis=-1, keepdims=True) + EPS) * gp_ref[...]
    o_ref[...] = x_ref[...] + gate_ref[0] * r


def _resident(shape):
    return pl.BlockSpec(shape, lambda i: (0,) * len(shape), pipeline_mode=pl.Buffered(1))


def _merge_out(oa, ob, oc, proj, gate_col, p_a, p_b, p_c, w_out, g_post, gate, x, seq, tm):
    n, d = x.shape
    w = oa.shape[1]
    tps = seq // tm

    def act():
        return pl.BlockSpec((tm, w), lambda i: (i, 0))

    def gcol(k):
        return pl.BlockSpec((tm, d), lambda i: (i, gate_col + k))

    return pl.pallas_call(
        _merge_kernel,
        grid=(n // tm,),
        in_specs=[act(), act(), act(), gcol(0), gcol(1), gcol(2),
                  _resident((w, d)), _resident((w, d)), _resident((w, d)), _resident((d, d)),
                  pl.BlockSpec((1, d), lambda i: (0, 0)),
                  pl.BlockSpec((1, 1, d), lambda i: (i // tps, 0, 0)),
                  pl.BlockSpec((tm, d), lambda i: (i, 0))],
        out_specs=pl.BlockSpec((tm, d), lambda i: (i, 0)),
        out_shape=jax.ShapeDtypeStruct((n, d), F32),
        compiler_params=_params(("parallel",)),
    )(oa, ob, oc, proj, proj, proj, p_a, p_b, p_c, w_out, g_post.reshape(1, d), gate, x)


def _up_gate_kernel(xp_ref, x_ref, xn_ref, g_ref, sh_ref, sc_ref, wa_ref, wb_ref, cwa_ref, cwb_ref,
                    cba_ref, cbb_ref, o_ref, h_ref, *, tps, m_split):
    tm = x_ref.shape[0]
    t = pl.program_id(0) % tps

    @pl.when(pl.program_id(1) == 0)
    def _():
        def norm_mod(x):
            y = x * lax.rsqrt(jnp.mean(x * x, axis=-1, keepdims=True) + EPS) * g_ref[...]
            return y * (1.0 + sc_ref[0]) + sh_ref[0]

        h_ref[HALO:HALO + tm, :] = norm_mod(x_ref[...]).astype(BF16)
        h_ref[0:HALO, :] = jnp.where(t == 0, 0.0, norm_mod(xp_ref[...])).astype(BF16)
        h_ref[HALO + tm:, :] = jnp.where(t == tps - 1, 0.0, norm_mod(xn_ref[...])).astype(BF16)

    n_ext = m_split + 2 * HALO
    for c in range(tm // m_split):
        r0 = c * m_split
        hc = h_ref[r0:r0 + n_ext, :]

        def half(w_ref, cw_ref, cb_ref):
            u = jnp.dot(hc, w_ref[...], preferred_element_type=F32)
            before = pltpu.roll(u, 1, axis=0)[HALO:HALO + m_split]
            after = pltpu.roll(u, n_ext - 1, axis=0)[HALO:HALO + m_split]
            return (before * cw_ref[0:1, :] + u[HALO:HALO + m_split] * cw_ref[1:2, :]
                    + after * cw_ref[2:3, :] + cb_ref[...])

        a = half(wa_ref, cwa_ref, cba_ref)
        b = half(wb_ref, cwb_ref, cbb_ref)
        o_ref[r0:r0 + m_split, :] = (a * _sigmoid(a) * b).astype(o_ref.dtype)


def _up_gate(x, g, shift, scale, w, conv_w, conv_b, seq, tm, tn, m_split):
    n, d = x.shape
    f2 = w.shape[1]
    f = f2 // 2
    kw = conv_w.shape[0]
    assert kw == 3 and f % tn == 0 and seq % tm == 0 and tm % m_split == 0 and tm % HALO == 0
    nc = f // tn
    tps = seq // tm
    r = tm // HALO
    last = n // HALO - 1
    cb = conv_b.reshape(1, f2)

    def wspec(o):
        return pl.BlockSpec((d, tn), lambda i, j: (0, j + o))

    def cwspec(o):
        return pl.BlockSpec((kw, tn), lambda i, j: (0, j + o))

    def cbspec(o):
        return pl.BlockSpec((1, tn), lambda i, j: (0, j + o))

    return pl.pallas_call(
        functools.partial(_up_gate_kernel, tps=tps, m_split=m_split),
        grid=(n // tm, nc),
        in_specs=[pl.BlockSpec((HALO, d), lambda i, j: (jnp.maximum(i * r - 1, 0), 0)),
                  pl.BlockSpec((tm, d), lambda i, j: (i, 0)),
                  pl.BlockSpec((HALO, d), lambda i, j: (jnp.minimum((i + 1) * r, last), 0)),
                  pl.BlockSpec((1, d), lambda i, j: (0, 0)),
                  pl.BlockSpec((1, 1, d), lambda i, j: (i // tps, 0, 0)),
                  pl.BlockSpec((1, 1, d), lambda i, j: (i // tps, 0, 0)),
                  wspec(0), wspec(nc), cwspec(0), cwspec(nc), cbspec(0), cbspec(nc)],
        out_specs=pl.BlockSpec((tm, tn), lambda i, j: (i, j)),
        out_shape=jax.ShapeDtypeStruct((n, f), BF16),
        scratch_shapes=[pltpu.VMEM((tm + 2 * HALO, d), BF16)],
        compiler_params=_params(("parallel", "arbitrary")),
    )(x, x, x, g.reshape(1, d), shift, scale, w, w, conv_w, conv_w, cb, cb)


def _down_kernel(h_ref, w_ref, gp_ref, gate_ref, x_ref, o_ref):
    y = jnp.dot(h_ref[...], w_ref[...], preferred_element_type=F32)
    r = y * lax.rsqrt(jnp.mean(y * y, axis=-1, keepdims=True) + EPS) * gp_ref[...]
    o_ref[...] = x_ref[...] + gate_ref[0] * r


def _down_norm_res(h, w, g_post, gate, x, seq, tm):
    n, d = x.shape
    f = h.shape[1]
    tps = seq // tm
    return pl.pallas_call(
        _down_kernel,
        grid=(n // tm,),
        in_specs=[pl.BlockSpec((tm, f), lambda i: (i, 0)),
                  _resident((f, d)),
                  pl.BlockSpec((1, d), lambda i: (0, 0)),
                  pl.BlockSpec((1, 1, d), lambda i: (i // tps, 0, 0)),
                  pl.BlockSpec((tm, d), lambda i: (i, 0))],
        out_specs=pl.BlockSpec((tm, d), lambda i: (i, 0)),
        out_shape=jax.ShapeDtypeStruct((n, d), F32),
        compiler_params=_params(("parallel",)),
    )(h, w, g_post.reshape(1, d), gate, x)


def kernel(x, c, ctx, c_ctx, w_mod, b_mod, g_pre_mix, w_in, na_rpb, conv_w, conv_b, conv_ln_g,
           conv_ln_b, lam_q1, lam_k1, lam_q2, lam_k2, diff_ln_g, p_a, p_b, p_c, w_out, g_post_mix,
           g_pre_ffn, w_up, ffn_conv_w, ffn_conv_b, w_down, g_post_ffn):
    b, seq, d = x.shape
    ctx_len = ctx.shape[1]
    depth = w_in.shape[0]
    na_w = NA_HEADS * HEAD_W
    conv_ch = conv_w.shape[2]
    qk_w = DIFF_HEADS * 2 * DIFF_QK_DIM
    u_off = 3 * na_w
    dq_off = u_off + 2 * conv_ch
    dv_off = dq_off + 2 * qk_w
    m_dv = dq_off
    m_gate = m_dv + DIFF_HEADS * HEAD_W
    na_q_col, na_k_col = 0, NA_HEADS
    tn = 1024
    assert m_gate % d == 0 and u_off % conv_ch == 0 and na_w == NA_HEADS * HEAD_W

    xs = x.reshape(b * seq, d)
    cs = ctx.reshape(b * ctx_len, d)
    cond = jnp.concatenate([c, c_ctx[None, :], jnp.zeros((8 - b - 1, d), F32)], axis=0)
    mods = _modulation(cond, w_mod, b_mod)
    rope_tabs = _rope_tables(seq)
    qscale = DIFF_QK_DIM ** -0.5 * math.log2(math.e)
    col_scale = jnp.concatenate([jnp.full((qk_w,), qscale, F32), jnp.ones((qk_w,), F32)])
    tk = 256
    tm_x, tm_c = 1024, ctx_len
    n_ck = seq // (NA_K_CHUNK * GRID_W)

    for l in range(depth):
        last = l == depth - 1
        lam_init = 0.8 - 0.6 * math.exp(-0.3 * l)
        mx = mods[l, :b].reshape(b, 6, 1, d)
        mc = jnp.broadcast_to(mods[l, b].reshape(1, 6, 1, d), (b, 6, 1, d))
        w_main = jnp.concatenate([w_in[l][:, :dq_off], w_in[l][:, dv_off:]], axis=1).astype(BF16)
        w_qk = w_in[l][:, dq_off:dv_off].astype(BF16)
        lams = (lam_q1[l], lam_k1[l], lam_q2[l], lam_k2[l])
        pa_l, pb_l, pc_l, wo_l = (p_a[l].astype(BF16), p_b[l].astype(BF16), p_c[l].astype(BF16),
                                  w_out[l].astype(BF16))
        w_up_l, w_down_l = w_up[l].astype(BF16), w_down[l].astype(BF16)

        proj = _norm_mod_matmul(xs, g_pre_mix[l], mx[:, 0], mx[:, 1], w_main, seq, tm_x, tn)
        qk = _norm_mod_matmul(xs, g_pre_mix[l], mx[:, 0], mx[:, 1], w_qk, seq, tm_x, tn,
                              col_scale=col_scale, rope_tabs=rope_tabs)
        cproj = _norm_mod_matmul(cs, g_pre_mix[l], mc[:, 0], mc[:, 1], w_main, ctx_len, tm_c, tn)
        cqk = _norm_mod_matmul(cs, g_pre_mix[l], mc[:, 0], mc[:, 1], w_qk, ctx_len, tm_c, tn,
                               col_scale=col_scale)
        bias = _na_bias_table(na_rpb[l])
        nv_t = (proj[:, 2 * na_w:3 * na_w].reshape(b, n_ck, NA_K_CHUNK * GRID_W, NA_HEADS, HEAD_W)
                .transpose(0, 3, 1, 4, 2))
        cnv_t = cproj[:, 2 * na_w:3 * na_w].reshape(b, ctx_len, NA_HEADS, HEAD_W).transpose(0, 2, 3, 1)
        oa = _na_latent(proj, cproj, nv_t, cnv_t, bias, b, seq, ctx_len, na_q_col, na_k_col)
        ob = _conformer(proj, conv_w[l], conv_b[l], conv_ln_g[l], conv_ln_b[l], seq, 512,
                        u_off // conv_ch, u_off // conv_ch + 1)
        ck = cqk[:, qk_w:].reshape(b, ctx_len, qk_w)
        cv = cproj[:, m_dv:m_gate].reshape(b, ctx_len, qk_w)
        k_rows = jnp.concatenate([qk[:, qk_w:].reshape(b, seq, qk_w), ck], axis=1)
        v_rows = jnp.concatenate([proj[:, m_dv:m_gate].reshape(b, seq, qk_w), cv], axis=1)
        k_all, vt_all = _kv_layout(k_rows, v_rows, b, _largest_tile(seq + ctx_len, DIFF_TK))
        oc = _diff_attn(qk, 0, k_all, vt_all, lams, diff_ln_g[l], lam_init, b, seq, 2048)
        if not last:
            oa_c = _dense_attn(cproj, b, ctx_len, na_q_col, na_k_col, 2 * NA_HEADS)
            ob_c = _conformer(cproj, conv_w[l], conv_b[l], conv_ln_g[l], conv_ln_b[l], ctx_len,
                              ctx_len, u_off // conv_ch, u_off // conv_ch + 1)
            kc_all, vtc_all = _kv_layout(ck, cv, b, tk)
            oc_c = _diff_attn(cqk, 0, kc_all, vtc_all, lams, diff_ln_g[l], lam_init, b, ctx_len, 256)
            cs = _merge_out(oa_c, ob_c, oc_c, cproj, m_gate // d, pa_l, pb_l, pc_l, wo_l,
                            g_post_mix[l], mc[:, 2], cs, ctx_len, 256)
        xs = _merge_out(oa, ob, oc, proj, m_gate // d, pa_l, pb_l, pc_l, wo_l, g_post_mix[l],
                        mx[:, 2], xs, seq, 256)

        hg = _up_gate(xs, g_pre_ffn[l], mx[:, 3], mx[:, 4], w_up_l, ffn_conv_w[l], ffn_conv_b[l],
                      seq, tm_x, 512, 1024)
        xs = _down_norm_res(hg, w_down_l, g_post_ffn[l], mx[:, 5], xs, seq, 256)
        if not last:
            hg = _up_gate(cs, g_pre_ffn[l], mc[:, 3], mc[:, 4], w_up_l, ffn_conv_w[l], ffn_conv_b[l],
                          ctx_len, tm_c, 512, tm_c)
            cs = _down_norm_res(hg, w_down_l, g_post_ffn[l], mc[:, 5], cs, ctx_len, 256)
    return xs.reshape(b, seq, d)
```

```python
import functools
import math

import numpy as np
import jax
import jax.numpy as jnp
from jax import lax
from jax.experimental import pallas as pl
from jax.experimental.pallas import tpu as pltpu

F32 = jnp.float32
BF16 = jnp.bfloat16

EPS = 1e-6
GRID_W = 64
NA_HEADS = 8
NA_WIN_ROWS = 8
NA_WIN_COLS = 16
DIFF_HEADS = 8
DIFF_QK_DIM = 64
HEAD_W = 128
SUBLANES = 8
ROPE_THETA = 10000.0
NEG = -1e30

NA_Q_ROWS = 8
NA_K_ROWS = 16
NA_K_CHUNK = 4
HALO = 16

NMM_M_SPLIT = 512
DIFF_TK = 1408

VMEM_LIMIT = 56 * 1024 * 1024
_NT = (((1,), (1,)), ((), ()))


def _params(sem):
    return pltpu.CompilerParams(dimension_semantics=sem, vmem_limit_bytes=VMEM_LIMIT)


def _sigmoid(x):
    return 1.0 / (1.0 + jnp.exp(-x))


def _mod_kernel(c_ref, w_ref, b_ref, o_ref):
    c = c_ref[...]
    h = (c * _sigmoid(c)).astype(BF16)
    o_ref[0] = jnp.dot(h, w_ref[0].astype(BF16), preferred_element_type=F32) + b_ref[0]


def _modulation(cond, w_mod, b_mod):
    nl, d, n = w_mod.shape
    tn = 1024
    return pl.pallas_call(
        _mod_kernel,
        grid=(nl, n // tn),
        in_specs=[pl.BlockSpec((8, d), lambda l, j: (0, 0)),
                  pl.BlockSpec((1, d, tn), lambda l, j: (l, 0, j)),
                  pl.BlockSpec((1, 1, tn), lambda l, j: (l, 0, j))],
        out_specs=pl.BlockSpec((1, 8, tn), lambda l, j: (l, 0, j)),
        out_shape=jax.ShapeDtypeStruct((nl, 8, n), F32),
        compiler_params=_params(("parallel", "parallel")),
    )(cond, w_mod, b_mod.reshape(nl, 1, n))


def _rope_block(blk, cos, s1, s2):
    return blk * cos + pltpu.roll(blk, HEAD_W - 16, axis=1) * s1 + pltpu.roll(blk, 16, axis=1) * s2


def _nmm_kernel(*refs, rope, scaled, m_split):
    x_ref, g_ref, sh_ref, sc_ref, w_ref = refs[:5]
    o_ref, h_ref = refs[-2:]
    extra = list(refs[5:-2])
    cs_ref = extra.pop(0) if scaled else None
    cos_ref, s1_ref, s2_ref = extra if rope else (None, None, None)

    @pl.when(pl.program_id(1) == 0)
    def _():
        x = x_ref[...]
        y = x * lax.rsqrt(jnp.mean(x * x, axis=-1, keepdims=True) + EPS) * g_ref[...]
        h_ref[...] = (y * (1.0 + sc_ref[0]) + sh_ref[0]).astype(BF16)

    tm, tn = o_ref.shape
    for mi in range(tm // m_split):
        rows = slice(mi * m_split, (mi + 1) * m_split)
        acc = jnp.dot(h_ref[rows, :], w_ref[...], preferred_element_type=F32)
        if not (rope or scaled):
            o_ref[rows, :] = acc.astype(o_ref.dtype)
            continue
        for hb in range(tn // HEAD_W):
            cols = slice(hb * HEAD_W, (hb + 1) * HEAD_W)
            blk = acc[:, cols]
            if rope:
                blk = _rope_block(blk, cos_ref[rows, :], s1_ref[rows, :], s2_ref[rows, :])
            if scaled:
                blk = blk * cs_ref[:, cols]
            o_ref[rows, cols] = blk.astype(o_ref.dtype)


def _norm_mod_matmul(x, g, shift, scale, w, seq, tm, tn, *, col_scale=None, rope_tabs=None):
    n, d = x.shape
    nout = w.shape[1]
    tps = seq // tm
    rope = rope_tabs is not None
    scaled = col_scale is not None
    in_specs = [pl.BlockSpec((tm, d), lambda i, j: (i, 0)),
                pl.BlockSpec((1, d), lambda i, j: (0, 0)),
                pl.BlockSpec((1, 1, d), lambda i, j: (i // tps, 0, 0)),
                pl.BlockSpec((1, 1, d), lambda i, j: (i // tps, 0, 0)),
                pl.BlockSpec((d, tn), lambda i, j: (0, j))]
    args = [x, g.reshape(1, d), shift, scale, w]
    if scaled:
        in_specs.append(pl.BlockSpec((1, tn), lambda i, j: (0, j)))
        args.append(col_scale.reshape(1, nout))
    if rope:
        in_specs += [pl.BlockSpec((tm, HEAD_W), lambda i, j: (i % tps, 0))] * 3
        args += list(rope_tabs)
    return pl.pallas_call(
        functools.partial(_nmm_kernel, rope=rope, scaled=scaled, m_split=min(tm, NMM_M_SPLIT)),
        grid=(n // tm, nout // tn),
        in_specs=in_specs,
        out_specs=pl.BlockSpec((tm, tn), lambda i, j: (i, j)),
        out_shape=jax.ShapeDtypeStruct((n, nout), BF16),
        scratch_shapes=[pltpu.VMEM((tm, d), BF16)],
        compiler_params=_params(("parallel", "arbitrary")),
    )(*args)


def _rope_tables(seq):
    t = jnp.arange(seq)
    rows, cols = (t // GRID_W).astype(F32), (t % GRID_W).astype(F32)
    half = DIFF_QK_DIM // 2
    inv = ROPE_THETA ** (-jnp.arange(0, half, 2, dtype=F32) / half)
    lane = np.arange(HEAD_W) % DIFF_QK_DIM
    use_col = lane >= half
    e = lane % half
    freq = e % (half // 2)
    second = e >= half // 2
    pos = jnp.where(jnp.asarray(use_col)[None, :], cols[:, None], rows[:, None])
    ang = pos * inv[freq][None, :]
    cos, sin = jnp.cos(ang), jnp.sin(ang)
    s1 = jnp.where(jnp.asarray(second)[None, :], 0.0, -sin)
    s2 = jnp.where(jnp.asarray(second)[None, :], sin, 0.0)
    return cos, s1, s2


def _na_kernel(q_ref, k0, k1, k2, k3, v0, v1, v2, v3, kc_ref, vc_ref, bias_ref, o_ref):
    q = q_ref[...]
    c = HEAD_W ** -0.5 * math.log2(math.e)
    ck = NA_K_CHUNK * GRID_W
    s = [lax.dot_general(kr[...], q, _NT, preferred_element_type=F32) * c
         + bias_ref[0, 0, j * ck:(j + 1) * ck, :].astype(F32) for j, kr in enumerate((k0, k1, k2, k3))]
    s.append(lax.dot_general(kc_ref[...], q, _NT, preferred_element_type=F32) * c)
    m = s[0].max(axis=0, keepdims=True)
    for t in s[1:]:
        m = jnp.maximum(m, t.max(axis=0, keepdims=True))
    p = [jnp.exp2(t - m) for t in s]
    l = p[0].sum(axis=0, keepdims=True)
    for t in p[1:]:
        l = l + t.sum(axis=0, keepdims=True)
    ot = jnp.dot(vc_ref[0, 0], p[4].astype(BF16), preferred_element_type=F32)
    for t, vr in zip(p[:4], (v0, v1, v2, v3)):
        ot = ot + jnp.dot(vr[0, 0, 0], t.astype(BF16), preferred_element_type=F32)
    o_ref[...] = (ot / l).T.astype(o_ref.dtype)


def _na_bias_table(rpb):
    i = np.arange(NA_Q_ROWS)[:, None]
    jr = np.arange(NA_K_ROWS)[None, :]
    half = NA_WIN_ROWS // 2
    lo_top = np.maximum(i - half, 0)
    lo_int = i + 0 * jr
    lo_bot = NA_Q_ROWS + np.minimum(i - half, 0)
    offs = (0, -half, -NA_Q_ROWS)
    dr, rv = [], []
    for off, lo in zip(offs, (lo_top, lo_int, lo_bot)):
        dr.append(off + jr - i + NA_WIN_ROWS - 1)
        rv.append((jr >= lo) & (jr < lo + NA_WIN_ROWS))
    dr, rv = np.stack(dr), np.stack(rv)
    qc = np.arange(GRID_W)[:, None]
    kc = np.arange(GRID_W)[None, :]
    c0 = np.clip(qc - NA_WIN_COLS // 2, 0, GRID_W - NA_WIN_COLS)
    dc = kc - qc + NA_WIN_COLS - 1
    cv = (kc >= c0) & (kc < c0 + NA_WIN_COLS)
    drc = np.clip(dr, 0, 2 * NA_WIN_ROWS - 2)
    dcc = np.clip(dc, 0, 2 * NA_WIN_COLS - 2)
    t = rpb.astype(F32)[:, drc]
    t = t[..., dcc]
    valid = rv[:, :, :, None, None] & cv[None, None, None]
    t = jnp.where(jnp.asarray(valid)[None], t * math.log2(math.e), NEG)
    t = t.transpose(1, 0, 3, 5, 2, 4)
    h = rpb.shape[0]
    return t.reshape(3, h, NA_K_ROWS * GRID_W, NA_Q_ROWS * GRID_W).astype(BF16)


def _na_latent(proj, cproj, vt, vct, bias, b, seq, ctx_len, q_col, k_col):
    tq = NA_Q_ROWS * GRID_W
    ck = NA_K_CHUNK * GRID_W
    n_rb = seq // tq
    n_ck = seq // ck
    assert seq % tq == 0 and n_rb >= 2
    n_chunks = NA_K_ROWS // NA_K_CHUNK

    def chunk0(rb):
        return jnp.clip(2 * rb - 1, 0, n_ck - n_chunks)

    def k_spec(j):
        return pl.BlockSpec((ck, HEAD_W), lambda h, rb, bi: (bi * n_ck + chunk0(rb) + j, k_col + h))

    def v_spec(j):
        return pl.BlockSpec((1, 1, 1, HEAD_W, ck), lambda h, rb, bi: (bi, h, chunk0(rb) + j, 0, 0))

    def variant(rb):
        return jnp.where(rb == 0, 0, jnp.where(rb == n_rb - 1, 2, 1))

    in_specs = ([pl.BlockSpec((tq, HEAD_W), lambda h, rb, bi: (bi * n_rb + rb, q_col + h))]
                + [k_spec(j) for j in range(n_chunks)]
                + [v_spec(j) for j in range(n_chunks)]
                + [pl.BlockSpec((ctx_len, HEAD_W), lambda h, rb, bi: (bi, k_col + h)),
                   pl.BlockSpec((1, 1, HEAD_W, ctx_len), lambda h, rb, bi: (bi, h, 0, 0)),
                   pl.BlockSpec((1, 1, NA_K_ROWS * GRID_W, tq), lambda h, rb, bi: (variant(rb), h, 0, 0))])
    return pl.pallas_call(
        _na_kernel,
        grid=(NA_HEADS, n_rb, b),
        in_specs=in_specs,
        out_specs=pl.BlockSpec((tq, HEAD_W), lambda h, rb, bi: (bi * n_rb + rb, h)),
        out_shape=jax.ShapeDtypeStruct((b * seq, NA_HEADS * HEAD_W), BF16),
        compiler_params=_params(("parallel", "parallel", "parallel")),
    )(*([proj] * (1 + n_chunks) + [vt] * n_chunks + [cproj, vct, bias]))


def _dense_attn_kernel(q_ref, k_ref, v_ref, o_ref):
    s = lax.dot_general(q_ref[...], k_ref[...], _NT, preferred_element_type=F32) * (HEAD_W ** -0.5)
    p = jnp.exp(s - s.max(axis=-1, keepdims=True))
    l = p.sum(axis=-1, keepdims=True)
    o = jnp.dot(p.astype(BF16), v_ref[...], preferred_element_type=F32)
    o_ref[...] = (o / l).astype(o_ref.dtype)


def _dense_attn(cproj, b, ctx_len, q_col, k_col, v_col):
    def spec(col):
        return pl.BlockSpec((ctx_len, HEAD_W), lambda bi, h: (bi, col + h))
    return pl.pallas_call(
        _dense_attn_kernel,
        grid=(b, NA_HEADS),
        in_specs=[spec(q_col), spec(k_col), spec(v_col)],
        out_specs=pl.BlockSpec((ctx_len, HEAD_W), lambda bi, h: (bi, h)),
        out_shape=jax.ShapeDtypeStruct((b * ctx_len, NA_HEADS * HEAD_W), BF16),
        compiler_params=_params(("parallel", "parallel")),
    )(cproj, cproj, cproj)


def _conformer_kernel(a_ref, g_ref, pa_ref, pg_ref, na_ref, ng_ref, w_ref, b_ref, lg_ref, lb_ref,
                      o_ref, ext_ref, sh_ref, *, tps, chunk):
    tm = a_ref.shape[0]
    kw = w_ref.shape[0]
    t = pl.program_id(0) % tps

    def glu(a, g):
        return a[...].astype(F32) * _sigmoid(g[...].astype(F32))

    ext_ref[HALO:HALO + tm, :] = glu(a_ref, g_ref)
    ext_ref[0:HALO, :] = jnp.where(t == 0, 0.0, glu(pa_ref, pg_ref))
    ext_ref[HALO + tm:, :] = jnp.where(t == tps - 1, 0.0, glu(na_ref, ng_ref))
    n_sh = sh_ref.shape[1]
    for ph in range(SUBLANES):
        sh_ref[ph] = ext_ref[ph:ph + n_sh, :]
    off = HALO - kw // 2
    for r in range(tm // chunk):
        acc = jnp.zeros((chunk, a_ref.shape[1]), F32) + b_ref[...]
        for k in range(kw):
            blk, ph = divmod(off + k, SUBLANES)
            row0 = r * chunk + blk * SUBLANES
            acc = acc + sh_ref[ph, row0:row0 + chunk, :] * w_ref[k:k + 1, :]
        mu = jnp.mean(acc, axis=-1, keepdims=True)
        xc = acc - mu
        y = xc * lax.rsqrt(jnp.mean(xc * xc, axis=-1, keepdims=True) + EPS) * lg_ref[...] + lb_ref[...]
        o_ref[r * chunk:(r + 1) * chunk, :] = (y * _sigmoid(y)).astype(o_ref.dtype)


def _conformer(proj, conv_w, conv_b, ln_g, ln_b, seq, tm, a_col, g_col):
    n = proj.shape[0]
    kw, ch = conv_w.shape
    assert kw // 2 < HALO and seq % tm == 0 and tm % HALO == 0
    tps = seq // tm
    r = tm // HALO
    last = n // HALO - 1

    def main(col):
        return pl.BlockSpec((tm, ch), lambda i: (i, col))

    def prev(col):
        return pl.BlockSpec((HALO, ch), lambda i: (jnp.maximum(i * r - 1, 0), col))

    def nxt(col):
        return pl.BlockSpec((HALO, ch), lambda i: (jnp.minimum((i + 1) * r, last), col))

    vec = pl.BlockSpec((1, ch), lambda i: (0, 0))
    return pl.pallas_call(
        functools.partial(_conformer_kernel, tps=tps, chunk=32),
        grid=(n // tm,),
        in_specs=[main(a_col), main(g_col), prev(a_col), prev(g_col), nxt(a_col), nxt(g_col),
                  pl.BlockSpec((kw, ch), lambda i: (0, 0)), vec, vec, vec],
        out_specs=pl.BlockSpec((tm, ch), lambda i: (i, 0)),
        out_shape=jax.ShapeDtypeStruct((n, ch), BF16),
        scratch_shapes=[pltpu.VMEM((tm + 2 * HALO, ch), F32),
                        pltpu.VMEM((SUBLANES, tm + 2 * HALO - SUBLANES, ch), F32)],
        compiler_params=_params(("parallel",)),
    )(proj, proj, proj, proj, proj, proj, conv_w, conv_b.reshape(1, ch), ln_g.reshape(1, ch),
      ln_b.reshape(1, ch))


def _diff_kernel(q_ref, k_ref, vt_ref, lq1_ref, lk1_ref, lq2_ref, lk2_ref, g_ref, o_ref,
                 qt_ref, s_ref, p_ref, mb_ref, m_ref, l_ref, al_ref, acc_ref, *, lam_init):
    tq = q_ref.shape[0]
    nkb = k_ref.shape[1]
    qt = q_ref[...].astype(F32).T
    row = lax.broadcasted_iota(jnp.int32, qt.shape, 0)
    qt_ref[:, :tq] = jnp.where(row < DIFF_QK_DIM, qt, 0.0).astype(BF16)
    qt_ref[:, tq:] = jnp.where(row >= DIFF_QK_DIM, qt, 0.0).astype(BF16)
    m_ref[...] = jnp.full(m_ref.shape, NEG, F32)
    l_ref[...] = jnp.zeros(l_ref.shape, F32)
    acc_ref[...] = jnp.zeros(acc_ref.shape, F32)

    def scores(kb):
        s = jnp.dot(k_ref[0, kb], qt_ref[...], preferred_element_type=F32)
        s_ref[...] = s
        mb_ref[...] = s.max(axis=0, keepdims=True)

    def softmax():
        for c in range(2 * tq // HEAD_W):
            cols = slice(c * HEAD_W, (c + 1) * HEAD_W)
            m_old = m_ref[:, cols]
            m_new = jnp.maximum(m_old, mb_ref[:, cols])
            alpha = jnp.exp2(m_old - m_new)
            p = jnp.exp2(s_ref[:, cols] - m_new)
            l_ref[:, cols] = alpha * l_ref[:, cols] + p.sum(axis=0, keepdims=True)
            m_ref[:, cols] = m_new
            al_ref[:, cols] = alpha
            p_ref[:, cols] = p.astype(BF16)

    def accumulate(kb):
        acc_ref[...] = al_ref[...] * acc_ref[...] + jnp.dot(vt_ref[0, 0, kb], p_ref[...],
                                                             preferred_element_type=F32)

    def tick(t, carry):
        accumulate(t - 2)
        softmax()
        scores(t)
        return carry

    scores(0)
    if nkb >= 2:
        softmax()
        scores(1)
        lax.fori_loop(2, nkb, tick, 0)
        accumulate(nkb - 2)
    softmax()
    accumulate(nkb - 1)

    lam = (jnp.exp(jnp.sum(lq1_ref[...] * lk1_ref[...], axis=-1, keepdims=True))
           - jnp.exp(jnp.sum(lq2_ref[...] * lk2_ref[...], axis=-1, keepdims=True)) + lam_init)
    o = acc_ref[...] / l_ref[...]
    ot = o[:, :tq] - lam * o[:, tq:]
    y = ot * lax.rsqrt(jnp.mean(ot * ot, axis=0, keepdims=True) + EPS) * g_ref[...]
    o_ref[...] = (y * (1.0 - lam_init)).T.astype(o_ref.dtype)


def _diff_attn(q_arr, q_col, k_all, vt_all, lams, ln_g, lam_init, b, seq, tq):
    nq = seq // tq
    nkb, tk = k_all.shape[1], k_all.shape[2]
    vec = pl.BlockSpec((1, DIFF_QK_DIM), lambda bi, h, qi: (0, 0))
    return pl.pallas_call(
        functools.partial(_diff_kernel, lam_init=lam_init),
        grid=(b, DIFF_HEADS, nq),
        in_specs=[pl.BlockSpec((tq, HEAD_W), lambda bi, h, qi: (bi * nq + qi, q_col + h)),
                  pl.BlockSpec((1, nkb, tk, HEAD_W), lambda bi, h, qi: (bi, 0, 0, h)),
                  pl.BlockSpec((1, 1, nkb, HEAD_W, tk), lambda bi, h, qi: (bi, h, 0, 0, 0)),
                  vec, vec, vec, vec,
                  pl.BlockSpec((HEAD_W, 1), lambda bi, h, qi: (0, 0))],
        out_specs=pl.BlockSpec((tq, HEAD_W), lambda bi, h, qi: (bi * nq + qi, h)),
        out_shape=jax.ShapeDtypeStruct((b * seq, DIFF_HEADS * HEAD_W), BF16),
        scratch_shapes=[pltpu.VMEM((HEAD_W, 2 * tq), BF16),
                        pltpu.VMEM((tk, 2 * tq), F32), pltpu.VMEM((tk, 2 * tq), BF16),
                        pltpu.VMEM((1, 2 * tq), F32),
                        pltpu.VMEM((1, 2 * tq), F32), pltpu.VMEM((1, 2 * tq), F32),
                        pltpu.VMEM((1, 2 * tq), F32), pltpu.VMEM((HEAD_W, 2 * tq), F32)],
        compiler_params=_params(("parallel", "parallel", "parallel")),
    )(q_arr, k_all, vt_all, *[v.reshape(1, DIFF_QK_DIM) for v in lams], ln_g.reshape(HEAD_W, 1))


def _largest_tile(n, limit):
    return max(t for t in range(HEAD_W, limit + 1, HEAD_W) if n % t == 0)


def _kv_layout(k_rows, v_rows, b, tk):
    lk = k_rows.shape[1]
    nkb = lk // tk
    k_all = k_rows.reshape(b, nkb, tk, DIFF_HEADS * HEAD_W)
    vt_all = v_rows.reshape(b, nkb, tk, DIFF_HEADS, HEAD_W).transpose(0, 3, 1, 4, 2)
    return k_all, vt_all


def _merge_kernel(oa_ref, ob_ref, oc_ref, g0_ref, g1_ref, g2_ref, pa_ref, pb_ref, pc_ref, wo_ref,
                  gp_ref, gate_ref, x_ref, o_ref):
    def branch(o_r, g_r, p_r):
        return _sigmoid(g_r[...].astype(F32)) * jnp.dot(o_r[...], p_r[...], preferred_element_type=F32)

    y = branch(oa_ref, g0_ref, pa_ref) + branch(ob_ref, g1_ref, pb_ref) + branch(oc_ref, g2_ref, pc_ref)
    mix = jnp.dot(y.astype(BF16), wo_ref[...], preferred_element_type=F32)
    r = mix * lax.rsqrt(jnp.mean(mix * mix, axis=-1, keepdims=True) + EPS) * gp_ref[...]
    o_ref[...] = x_ref[...] + gate_ref[0] * r


def _resident(shape):
    return pl.BlockSpec(shape, lambda i: (0,) * len(shape), pipeline_mode=pl.Buffered(1))


def _merge_out(oa, ob, oc, proj, gate_col, p_a, p_b, p_c, w_out, g_post, gate, x, seq, tm):
    n, d = x.shape
    w = oa.shape[1]
    tps = seq // tm

    def act():
        return pl.BlockSpec((tm, w), lambda i: (i, 0))

    def gcol(k):
        return pl.BlockSpec((tm, d), lambda i: (i, gate_col + k))

    return pl.pallas_call(
        _merge_kernel,
        grid=(n // tm,),
        in_specs=[act(), act(), act(), gcol(0), gcol(1), gcol(2),
                  _resident((w, d)), _resident((w, d)), _resident((w, d)), _resident((d, d)),
                  pl.BlockSpec((1, d), lambda i: (0, 0)),
                  pl.BlockSpec((1, 1, d), lambda i: (i // tps, 0, 0)),
                  pl.BlockSpec((tm, d), lambda i: (i, 0))],
        out_specs=pl.BlockSpec((tm, d), lambda i: (i, 0)),
        out_shape=jax.ShapeDtypeStruct((n, d), F32),
        compiler_params=_params(("parallel",)),
    )(oa, ob, oc, proj, proj, proj, p_a, p_b, p_c, w_out, g_post.reshape(1, d), gate, x)


def _up_gate_kernel(xp_ref, x_ref, xn_ref, g_ref, sh_ref, sc_ref, wa_ref, wb_ref, cwa_ref, cwb_ref,
                    cba_ref, cbb_ref, o_ref, h_ref, *, tps, m_split):
    tm = x_ref.shape[0]
    t = pl.program_id(0) % tps

    @pl.when(pl.program_id(1) == 0)
    def _():
        def norm_mod(x):
            y = x * lax.rsqrt(jnp.mean(x * x, axis=-1, keepdims=True) + EPS) * g_ref[...]
            return y * (1.0 + sc_ref[0]) + sh_ref[0]

        h_ref[HALO:HALO + tm, :] = norm_mod(x_ref[...]).astype(BF16)
        h_ref[0:HALO, :] = jnp.where(t == 0, 0.0, norm_mod(xp_ref[...])).astype(BF16)
        h_ref[HALO + tm:, :] = jnp.where(t == tps - 1, 0.0, norm_mod(xn_ref[...])).astype(BF16)

    n_ext = m_split + 2 * HALO
    for c in range(tm // m_split):
        r0 = c * m_split
        hc = h_ref[r0:r0 + n_ext, :]

        def half(w_ref, cw_ref, cb_ref):
            u = jnp.dot(hc, w_ref[...], preferred_element_type=F32)
            before = pltpu.roll(u, 1, axis=0)[HALO:HALO + m_split]
            after = pltpu.roll(u, n_ext - 1, axis=0)[HALO:HALO + m_split]
            return (before * cw_ref[0:1, :] + u[HALO:HALO + m_split] * cw_ref[1:2, :]
                    + after * cw_ref[2:3, :] + cb_ref[...])

        a = half(wa_ref, cwa_ref, cba_ref)
        b = half(wb_ref, cwb_ref, cbb_ref)
        o_ref[r0:r0 + m_split, :] = (a * _sigmoid(a) * b).astype(o_ref.dtype)


def _up_gate(x, g, shift, scale, w, conv_w, conv_b, seq, tm, tn, m_split):
    n, d = x.shape
    f2 = w.shape[1]
    f = f2 // 2
    kw = conv_w.shape[0]
    assert kw == 3 and f % tn == 0 and seq % tm == 0 and tm % m_split == 0 and tm % HALO == 0
    nc = f // tn
    tps = seq // tm
    r = tm // HALO
    last = n // HALO - 1
    cb = conv_b.reshape(1, f2)

    def wspec(o):
        return pl.BlockSpec((d, tn), lambda i, j: (0, j + o))

    def cwspec(o):
        return pl.BlockSpec((kw, tn), lambda i, j: (0, j + o))

    def cbspec(o):
        return pl.BlockSpec((1, tn), lambda i, j: (0, j + o))

    return pl.pallas_call(
        functools.partial(_up_gate_kernel, tps=tps, m_split=m_split),
        grid=(n // tm, nc),
        in_specs=[pl.BlockSpec((HALO, d), lambda i, j: (jnp.maximum(i * r - 1, 0), 0)),
                  pl.BlockSpec((tm, d), lambda i, j: (i, 0)),
                  pl.BlockSpec((HALO, d), lambda i, j: (jnp.minimum((i + 1) * r, last), 0)),
                  pl.BlockSpec((1, d), lambda i, j: (0, 0)),
                  pl.BlockSpec((1, 1, d), lambda i, j: (i // tps, 0, 0)),
                  pl.BlockSpec((1, 1, d), lambda i, j: (i // tps, 0, 0)),
                  wspec(0), wspec(nc), cwspec(0), cwspec(nc), cbspec(0), cbspec(nc)],
        out_specs=pl.BlockSpec((tm, tn), lambda i, j: (i, j)),
        out_shape=jax.ShapeDtypeStruct((n, f), BF16),
        scratch_shapes=[pltpu.VMEM((tm + 2 * HALO, d), BF16)],
        compiler_params=_params(("parallel", "arbitrary")),
    )(x, x, x, g.reshape(1, d), shift, scale, w, w, conv_w, conv_w, cb, cb)


def _down_kernel(h_ref, w_ref, gp_ref, gate_ref, x_ref, o_ref):
    y = jnp.dot(h_ref[...], w_ref[...], preferred_element_type=F32)
    r = y * lax.rsqrt(jnp.mean(y * y, axis=-1, keepdims=True) + EPS) * gp_ref[...]
    o_ref[...] = x_ref[...] + gate_ref[0] * r


def _down_norm_res(h, w, g_post, gate, x, seq, tm):
    n, d = x.shape
    f = h.shape[1]
    tps = seq // tm
    return pl.pallas_call(
        _down_kernel,
        grid=(n // tm,),
        in_specs=[pl.BlockSpec((tm, f), lambda i: (i, 0)),
                  _resident((f, d)),
                  pl.BlockSpec((1, d), lambda i: (0, 0)),
                  pl.BlockSpec((1, 1, d), lambda i: (i // tps, 0, 0)),
                  pl.BlockSpec((tm, d), lambda i: (i, 0))],
        out_specs=pl.BlockSpec((tm, d), lambda i: (i, 0)),
        out_shape=jax.ShapeDtypeStruct((n, d), F32),
        compiler_params=_params(("parallel",)),
    )(h, w, g_post.reshape(1, d), gate, x)


def kernel(x, c, ctx, c_ctx, w_mod, b_mod, g_pre_mix, w_in, na_rpb, conv_w, conv_b, conv_ln_g,
           conv_ln_b, lam_q1, lam_k1, lam_q2, lam_k2, diff_ln_g, p_a, p_b, p_c, w_out, g_post_mix,
           g_pre_ffn, w_up, ffn_conv_w, ffn_conv_b, w_down, g_post_ffn):
    b, seq, d = x.shape
    ctx_len = ctx.shape[1]
    depth = w_in.shape[0]
    na_w = NA_HEADS * HEAD_W
    conv_ch = conv_w.shape[2]
    qk_w = DIFF_HEADS * 2 * DIFF_QK_DIM
    u_off = 3 * na_w
    dq_off = u_off + 2 * conv_ch
    dv_off = dq_off + 2 * qk_w
    m_dv = dq_off
    m_gate = m_dv + DIFF_HEADS * HEAD_W
    na_q_col, na_k_col = 0, NA_HEADS
    tn = 1024
    assert m_gate % d == 0 and u_off % conv_ch == 0 and na_w == NA_HEADS * HEAD_W

    xs = x.reshape(b * seq, d)
    cs = ctx.reshape(b * ctx_len, d)
    cond = jnp.concatenate([c, c_ctx[None, :], jnp.zeros((8 - b - 1, d), F32)], axis=0)
    mods = _modulation(cond, w_mod, b_mod)
    rope_tabs = _rope_tables(seq)
    qscale = DIFF_QK_DIM ** -0.5 * math.log2(math.e)
    col_scale = jnp.concatenate([jnp.full((qk_w,), qscale, F32), jnp.ones((qk_w,), F32)])
    tk = 256
    tm_x, tm_c = 1024, ctx_len
    n_ck = seq // (NA_K_CHUNK * GRID_W)

    for l in range(depth):
        last = l == depth - 1
        lam_init = 0.8 - 0.6 * math.exp(-0.3 * l)
        mx = mods[l, :b].reshape(b, 6, 1, d)
        mc = jnp.broadcast_to(mods[l, b].reshape(1, 6, 1, d), (b, 6, 1, d))
        w_main = jnp.concatenate([w_in[l][:, :dq_off], w_in[l][:, dv_off:]], axis=1).astype(BF16)
        w_qk = w_in[l][:, dq_off:dv_off].astype(BF16)
        lams = (lam_q1[l], lam_k1[l], lam_q2[l], lam_k2[l])
        pa_l, pb_l, pc_l, wo_l = (p_a[l].astype(BF16), p_b[l].astype(BF16), p_c[l].astype(BF16),
                                  w_out[l].astype(BF16))
        w_up_l, w_down_l = w_up[l].astype(BF16), w_down[l].astype(BF16)

        proj = _norm_mod_matmul(xs, g_pre_mix[l], mx[:, 0], mx[:, 1], w_main, seq, tm_x, 2 * tn)
        qk = _norm_mod_matmul(xs, g_pre_mix[l], mx[:, 0], mx[:, 1], w_qk, seq, tm_x, 2 * tn,
                              col_scale=col_scale, rope_tabs=rope_tabs)
        cproj = _norm_mod_matmul(cs, g_pre_mix[l], mc[:, 0], mc[:, 1], w_main, ctx_len, tm_c, tn)
        cqk = _norm_mod_matmul(cs, g_pre_mix[l], mc[:, 0], mc[:, 1], w_qk, ctx_len, tm_c, tn,
                               col_scale=col_scale)
        bias = _na_bias_table(na_rpb[l])
        nv_t = (proj[:, 2 * na_w:3 * na_w].reshape(b, n_ck, NA_K_CHUNK * GRID_W, NA_HEADS, HEAD_W)
                .transpose(0, 3, 1, 4, 2))
        cnv_t = cproj[:, 2 * na_w:3 * na_w].reshape(b, ctx_len, NA_HEADS, HEAD_W).transpose(0, 2, 3, 1)
        oa = _na_latent(proj, cproj, nv_t, cnv_t, bias, b, seq, ctx_len, na_q_col, na_k_col)
        ob = _conformer(proj, conv_w[l], conv_b[l], conv_ln_g[l], conv_ln_b[l], seq, 512,
                        u_off // conv_ch, u_off // conv_ch + 1)
        ck = cqk[:, qk_w:].reshape(b, ctx_len, qk_w)
        cv = cproj[:, m_dv:m_gate].reshape(b, ctx_len, qk_w)
        k_rows = jnp.concatenate([qk[:, qk_w:].reshape(b, seq, qk_w), ck], axis=1)
        v_rows = jnp.concatenate([proj[:, m_dv:m_gate].reshape(b, seq, qk_w), cv], axis=1)
        k_all, vt_all = _kv_layout(k_rows, v_rows, b, _largest_tile(seq + ctx_len, DIFF_TK))
        oc = _diff_attn(qk, 0, k_all, vt_all, lams, diff_ln_g[l], lam_init, b, seq, 2048)
        if not last:
            oa_c = _dense_attn(cproj, b, ctx_len, na_q_col, na_k_col, 2 * NA_HEADS)
            ob_c = _conformer(cproj, conv_w[l], conv_b[l], conv_ln_g[l], conv_ln_b[l], ctx_len,
                              ctx_len, u_off // conv_ch, u_off // conv_ch + 1)
            kc_all, vtc_all = _kv_layout(ck, cv, b, tk)
            oc_c = _diff_attn(cqk, 0, kc_all, vtc_all, lams, diff_ln_g[l], lam_init, b, ctx_len, 256)
            cs = _merge_out(oa_c, ob_c, oc_c, cproj, m_gate // d, pa_l, pb_l, pc_l, wo_l,
                            g_post_mix[l], mc[:, 2], cs, ctx_len, 256)
        xs = _merge_out(oa, ob, oc, proj, m_gate // d, pa_l, pb_l, pc_l, wo_l, g_post_mix[l],
                        mx[:, 2], xs, seq, 256)

        hg = _up_gate(xs, g_pre_ffn[l], mx[:, 3], mx[:, 4], w_up_l, ffn_conv_w[l], ffn_conv_b[l],
                      seq, tm_x, 512, 1024)
        xs = _down_norm_res(hg, w_down_l, g_post_ffn[l], mx[:, 5], xs, seq, 256)
        if not last:
            hg = _up_gate(cs, g_pre_ffn[l], mc[:, 3], mc[:, 4], w_up_l, ffn_conv_w[l], ffn_conv_b[l],
                          ctx_len, tm_c, 512, tm_c)
            cs = _down_norm_res(hg, w_down_l, g_post_ffn[l], mc[:, 5], cs, ctx_len, 256)
    return xs.reshape(b, seq, d)
```

```python
import functools
import math

import numpy as np
import jax
import jax.numpy as jnp
from jax import lax
from jax.experimental import pallas as pl
from jax.experimental.pallas import tpu as pltpu

F32 = jnp.float32
BF16 = jnp.bfloat16

EPS = 1e-6
GRID_W = 64
NA_HEADS = 8
NA_WIN_ROWS = 8
NA_WIN_COLS = 16
DIFF_HEADS = 8
DIFF_QK_DIM = 64
HEAD_W = 128
SUBLANES = 8
ROPE_THETA = 10000.0
NEG = -1e30

NA_Q_ROWS = 8
NA_K_ROWS = 16
NA_K_CHUNK = 4
HALO = 16

NMM_M_SPLIT = 512
DIFF_TK = 1408

VMEM_LIMIT = 56 * 1024 * 1024
_NT = (((1,), (1,)), ((), ()))


def _params(sem):
    return pltpu.CompilerParams(dimension_semantics=sem, vmem_limit_bytes=VMEM_LIMIT)


def _sigmoid(x):
    return 1.0 / (1.0 + jnp.exp(-x))


def _mod_kernel(c_ref, w_ref, b_ref, o_ref):
    c = c_ref[...]
    h = (c * _sigmoid(c)).astype(BF16)
    o_ref[0] = jnp.dot(h, w_ref[0].astype(BF16), preferred_element_type=F32) + b_ref[0]


def _modulation(cond, w_mod, b_mod):
    nl, d, n = w_mod.shape
    tn = 1024
    return pl.pallas_call(
        _mod_kernel,
        grid=(nl, n // tn),
        in_specs=[pl.BlockSpec((8, d), lambda l, j: (0, 0)),
                  pl.BlockSpec((1, d, tn), lambda l, j: (l, 0, j)),
                  pl.BlockSpec((1, 1, tn), lambda l, j: (l, 0, j))],
        out_specs=pl.BlockSpec((1, 8, tn), lambda l, j: (l, 0, j)),
        out_shape=jax.ShapeDtypeStruct((nl, 8, n), F32),
        compiler_params=_params(("parallel", "parallel")),
    )(cond, w_mod, b_mod.reshape(nl, 1, n))


def _rope_block(blk, cos, s1, s2):
    return blk * cos + pltpu.roll(blk, HEAD_W - 16, axis=1) * s1 + pltpu.roll(blk, 16, axis=1) * s2


def _nmm_kernel(*refs, rope, scaled, m_split):
    x_ref, g_ref, sh_ref, sc_ref, w_ref = refs[:5]
    o_ref, h_ref = refs[-2:]
    extra = list(refs[5:-2])
    cs_ref = extra.pop(0) if scaled else None
    cos_ref, s1_ref, s2_ref = extra if rope else (None, None, None)

    @pl.when(pl.program_id(1) == 0)
    def _():
        x = x_ref[...]
        y = x * lax.rsqrt(jnp.mean(x * x, axis=-1, keepdims=True) + EPS) * g_ref[...]
        h_ref[...] = (y * (1.0 + sc_ref[0]) + sh_ref[0]).astype(BF16)

    tm, tn = o_ref.shape
    for mi in range(tm // m_split):
        rows = slice(mi * m_split, (mi + 1) * m_split)
        acc = jnp.dot(h_ref[rows, :], w_ref[...], preferred_element_type=F32)
        if not (rope or scaled):
            o_ref[rows, :] = acc.astype(o_ref.dtype)
            continue
        for hb in range(tn // HEAD_W):
            cols = slice(hb * HEAD_W, (hb + 1) * HEAD_W)
            blk = acc[:, cols]
            if rope:
                blk = _rope_block(blk, cos_ref[rows, :], s1_ref[rows, :], s2_ref[rows, :])
            if scaled:
                blk = blk * cs_ref[:, cols]
            o_ref[rows, cols] = blk.astype(o_ref.dtype)


def _norm_mod_matmul(x, g, shift, scale, w, seq, tm, tn, *, col_scale=None, rope_tabs=None):
    n, d = x.shape
    nout = w.shape[1]
    tps = seq // tm
    rope = rope_tabs is not None
    scaled = col_scale is not None
    in_specs = [pl.BlockSpec((tm, d), lambda i, j: (i, 0)),
                pl.BlockSpec((1, d), lambda i, j: (0, 0)),
                pl.BlockSpec((1, 1, d), lambda i, j: (i // tps, 0, 0)),
                pl.BlockSpec((1, 1, d), lambda i, j: (i // tps, 0, 0)),
                pl.BlockSpec((d, tn), lambda i, j: (0, j))]
    args = [x, g.reshape(1, d), shift, scale, w]
    if scaled:
        in_specs.append(pl.BlockSpec((1, tn), lambda i, j: (0, j)))
        args.append(col_scale.reshape(1, nout))
    if rope:
        in_specs += [pl.BlockSpec((tm, HEAD_W), lambda i, j: (i % tps, 0))] * 3
        args += list(rope_tabs)
    return pl.pallas_call(
        functools.partial(_nmm_kernel, rope=rope, scaled=scaled, m_split=min(tm, NMM_M_SPLIT)),
        grid=(n // tm, nout // tn),
        in_specs=in_specs,
        out_specs=pl.BlockSpec((tm, tn), lambda i, j: (i, j)),
        out_shape=jax.ShapeDtypeStruct((n, nout), BF16),
        scratch_shapes=[pltpu.VMEM((tm, d), BF16)],
        compiler_params=_params(("parallel", "arbitrary")),
    )(*args)


def _rope_tables(seq):
    t = jnp.arange(seq)
    rows, cols = (t // GRID_W).astype(F32), (t % GRID_W).astype(F32)
    half = DIFF_QK_DIM // 2
    inv = ROPE_THETA ** (-jnp.arange(0, half, 2, dtype=F32) / half)
    lane = np.arange(HEAD_W) % DIFF_QK_DIM
    use_col = lane >= half
    e = lane % half
    freq = e % (half // 2)
    second = e >= half // 2
    pos = jnp.where(jnp.asarray(use_col)[None, :], cols[:, None], rows[:, None])
    ang = pos * inv[freq][None, :]
    cos, sin = jnp.cos(ang), jnp.sin(ang)
    s1 = jnp.where(jnp.asarray(second)[None, :], 0.0, -sin)
    s2 = jnp.where(jnp.asarray(second)[None, :], sin, 0.0)
    return cos, s1, s2


def _na_kernel(q_ref, k0, k1, k2, k3, v0, v1, v2, v3, kc_ref, vc_ref, bias_ref, o_ref):
    q = q_ref[...]
    c = HEAD_W ** -0.5 * math.log2(math.e)
    ck = NA_K_CHUNK * GRID_W
    s = [lax.dot_general(kr[...], q, _NT, preferred_element_type=F32) * c
         + bias_ref[0, 0, j * ck:(j + 1) * ck, :].astype(F32) for j, kr in enumerate((k0, k1, k2, k3))]
    s.append(lax.dot_general(kc_ref[...], q, _NT, preferred_element_type=F32) * c)
    m = s[0].max(axis=0, keepdims=True)
    for t in s[1:]:
        m = jnp.maximum(m, t.max(axis=0, keepdims=True))
    p = [jnp.exp2(t - m) for t in s]
    l = p[0].sum(axis=0, keepdims=True)
    for t in p[1:]:
        l = l + t.sum(axis=0, keepdims=True)
    ot = jnp.dot(vc_ref[0, 0], p[4].astype(BF16), preferred_element_type=F32)
    for t, vr in zip(p[:4], (v0, v1, v2, v3)):
        ot = ot + jnp.dot(vr[0, 0, 0], t.astype(BF16), preferred_element_type=F32)
    o_ref[...] = (ot / l).T.astype(o_ref.dtype)


def _na_bias_table(rpb):
    i = np.arange(NA_Q_ROWS)[:, None]
    jr = np.arange(NA_K_ROWS)[None, :]
    half = NA_WIN_ROWS // 2
    lo_top = np.maximum(i - half, 0)
    lo_int = i + 0 * jr
    lo_bot = NA_Q_ROWS + np.minimum(i - half, 0)
    offs = (0, -half, -NA_Q_ROWS)
    dr, rv = [], []
    for off, lo in zip(offs, (lo_top, lo_int, lo_bot)):
        dr.append(off + jr - i + NA_WIN_ROWS - 1)
        rv.append((jr >= lo) & (jr < lo + NA_WIN_ROWS))
    dr, rv = np.stack(dr), np.stack(rv)
    qc = np.arange(GRID_W)[:, None]
    kc = np.arange(GRID_W)[None, :]
    c0 = np.clip(qc - NA_WIN_COLS // 2, 0, GRID_W - NA_WIN_COLS)
    dc = kc - qc + NA_WIN_COLS - 1
    cv = (kc >= c0) & (kc < c0 + NA_WIN_COLS)
    drc = np.clip(dr, 0, 2 * NA_WIN_ROWS - 2)
    dcc = np.clip(dc, 0, 2 * NA_WIN_COLS - 2)
    t = rpb.astype(F32)[:, drc]
    t = t[..., dcc]
    valid = rv[:, :, :, None, None] & cv[None, None, None]
    t = jnp.where(jnp.asarray(valid)[None], t * math.log2(math.e), NEG)
    t = t.transpose(1, 0, 3, 5, 2, 4)
    h = rpb.shape[0]
    return t.reshape(3, h, NA_K_ROWS * GRID_W, NA_Q_ROWS * GRID_W).astype(BF16)


def _na_latent(proj, cproj, vt, vct, bias, b, seq, ctx_len, q_col, k_col):
    tq = NA_Q_ROWS * GRID_W
    ck = NA_K_CHUNK * GRID_W
    n_rb = seq // tq
    n_ck = seq // ck
    assert seq % tq == 0 and n_rb >= 2
    n_chunks = NA_K_ROWS // NA_K_CHUNK

    def chunk0(rb):
        return jnp.clip(2 * rb - 1, 0, n_ck - n_chunks)

    def k_spec(j):
        return pl.BlockSpec((ck, HEAD_W), lambda h, rb, bi: (bi * n_ck + chunk0(rb) + j, k_col + h))

    def v_spec(j):
        return pl.BlockSpec((1, 1, 1, HEAD_W, ck), lambda h, rb, bi: (bi, h, chunk0(rb) + j, 0, 0))

    def variant(rb):
        return jnp.where(rb == 0, 0, jnp.where(rb == n_rb - 1, 2, 1))

    in_specs = ([pl.BlockSpec((tq, HEAD_W), lambda h, rb, bi: (bi * n_rb + rb, q_col + h))]
                + [k_spec(j) for j in range(n_chunks)]
                + [v_spec(j) for j in range(n_chunks)]
                + [pl.BlockSpec((ctx_len, HEAD_W), lambda h, rb, bi: (bi, k_col + h)),
                   pl.BlockSpec((1, 1, HEAD_W, ctx_len), lambda h, rb, bi: (bi, h, 0, 0)),
                   pl.BlockSpec((1, 1, NA_K_ROWS * GRID_W, tq), lambda h, rb, bi: (variant(rb), h, 0, 0))])
    return pl.pallas_call(
        _na_kernel,
        grid=(NA_HEADS, n_rb, b),
        in_specs=in_specs,
        out_specs=pl.BlockSpec((tq, HEAD_W), lambda h, rb, bi: (bi * n_rb + rb, h)),
        out_shape=jax.ShapeDtypeStruct((b * seq, NA_HEADS * HEAD_W), BF16),
        compiler_params=_params(("parallel", "parallel", "parallel")),
    )(*([proj] * (1 + n_chunks) + [vt] * n_chunks + [cproj, vct, bias]))


def _dense_attn_kernel(q_ref, k_ref, v_ref, o_ref):
    s = lax.dot_general(q_ref[...], k_ref[...], _NT, preferred_element_type=F32) * (HEAD_W ** -0.5)
    p = jnp.exp(s - s.max(axis=-1, keepdims=True))
    l = p.sum(axis=-1, keepdims=True)
    o = jnp.dot(p.astype(BF16), v_ref[...], preferred_element_type=F32)
    o_ref[...] = (o / l).astype(o_ref.dtype)


def _dense_attn(cproj, b, ctx_len, q_col, k_col, v_col):
    def spec(col):
        return pl.BlockSpec((ctx_len, HEAD_W), lambda bi, h: (bi, col + h))
    return pl.pallas_call(
        _dense_attn_kernel,
        grid=(b, NA_HEADS),
        in_specs=[spec(q_col), spec(k_col), spec(v_col)],
        out_specs=pl.BlockSpec((ctx_len, HEAD_W), lambda bi, h: (bi, h)),
        out_shape=jax.ShapeDtypeStruct((b * ctx_len, NA_HEADS * HEAD_W), BF16),
        compiler_params=_params(("parallel", "parallel")),
    )(cproj, cproj, cproj)


def _conformer_kernel(a_ref, g_ref, pa_ref, pg_ref, na_ref, ng_ref, w_ref, b_ref, lg_ref, lb_ref,
                      o_ref, ext_ref, sh_ref, *, tps, chunk):
    tm = a_ref.shape[0]
    kw = w_ref.shape[0]
    t = pl.program_id(0) % tps

    def glu(a, g):
        return a[...].astype(F32) * _sigmoid(g[...].astype(F32))

    ext_ref[HALO:HALO + tm, :] = glu(a_ref, g_ref)
    ext_ref[0:HALO, :] = jnp.where(t == 0, 0.0, glu(pa_ref, pg_ref))
    ext_ref[HALO + tm:, :] = jnp.where(t == tps - 1, 0.0, glu(na_ref, ng_ref))
    n_sh = sh_ref.shape[1]
    for ph in range(SUBLANES):
        sh_ref[ph] = ext_ref[ph:ph + n_sh, :]
    off = HALO - kw // 2
    for r in range(tm // chunk):
        acc = jnp.zeros((chunk, a_ref.shape[1]), F32) + b_ref[...]
        for k in range(kw):
            blk, ph = divmod(off + k, SUBLANES)
            row0 = r * chunk + blk * SUBLANES
            acc = acc + sh_ref[ph, row0:row0 + chunk, :] * w_ref[k:k + 1, :]
        mu = jnp.mean(acc, axis=-1, keepdims=True)
        xc = acc - mu
        y = xc * lax.rsqrt(jnp.mean(xc * xc, axis=-1, keepdims=True) + EPS) * lg_ref[...] + lb_ref[...]
        o_ref[r * chunk:(r + 1) * chunk, :] = (y * _sigmoid(y)).astype(o_ref.dtype)


def _conformer(proj, conv_w, conv_b, ln_g, ln_b, seq, tm, a_col, g_col):
    n = proj.shape[0]
    kw, ch = conv_w.shape
    assert kw // 2 < HALO and seq % tm == 0 and tm % HALO == 0
    tps = seq // tm
    r = tm // HALO
    last = n // HALO - 1

    def main(col):
        return pl.BlockSpec((tm, ch), lambda i: (i, col))

    def prev(col):
        return pl.BlockSpec((HALO, ch), lambda i: (jnp.maximum(i * r - 1, 0), col))

    def nxt(col):
        return pl.BlockSpec((HALO, ch), lambda i: (jnp.minimum((i + 1) * r, last), col))

    vec = pl.BlockSpec((1, ch), lambda i: (0, 0))
    return pl.pallas_call(
        functools.partial(_conformer_kernel, tps=tps, chunk=32),
        grid=(n // tm,),
        in_specs=[main(a_col), main(g_col), prev(a_col), prev(g_col), nxt(a_col), nxt(g_col),
                  pl.BlockSpec((kw, ch), lambda i: (0, 0)), vec, vec, vec],
        out_specs=pl.BlockSpec((tm, ch), lambda i: (i, 0)),
        out_shape=jax.ShapeDtypeStruct((n, ch), BF16),
        scratch_shapes=[pltpu.VMEM((tm + 2 * HALO, ch), F32),
                        pltpu.VMEM((SUBLANES, tm + 2 * HALO - SUBLANES, ch), F32)],
        compiler_params=_params(("parallel",)),
    )(proj, proj, proj, proj, proj, proj, conv_w, conv_b.reshape(1, ch), ln_g.reshape(1, ch),
      ln_b.reshape(1, ch))


def _diff_kernel(q_ref, k_ref, vt_ref, lq1_ref, lk1_ref, lq2_ref, lk2_ref, g_ref, o_ref,
                 qt_ref, s_ref, p_ref, mb_ref, m_ref, l_ref, al_ref, acc_ref, *, lam_init):
    tq = q_ref.shape[0]
    nkb = k_ref.shape[1]
    qt = q_ref[...].astype(F32).T
    row = lax.broadcasted_iota(jnp.int32, qt.shape, 0)
    qt_ref[:, :tq] = jnp.where(row < DIFF_QK_DIM, qt, 0.0).astype(BF16)
    qt_ref[:, tq:] = jnp.where(row >= DIFF_QK_DIM, qt, 0.0).astype(BF16)
    m_ref[...] = jnp.full(m_ref.shape, NEG, F32)
    l_ref[...] = jnp.zeros(l_ref.shape, F32)
    acc_ref[...] = jnp.zeros(acc_ref.shape, F32)

    def scores(kb):
        s = jnp.dot(k_ref[0, kb], qt_ref[...], preferred_element_type=F32)
        s_ref[...] = s
        mb_ref[...] = s.max(axis=0, keepdims=True)

    def softmax():
        for c in range(2 * tq // HEAD_W):
            cols = slice(c * HEAD_W, (c + 1) * HEAD_W)
            m_old = m_ref[:, cols]
            m_new = jnp.maximum(m_old, mb_ref[:, cols])
            alpha = jnp.exp2(m_old - m_new)
            p = jnp.exp2(s_ref[:, cols] - m_new)
            l_ref[:, cols] = alpha * l_ref[:, cols] + p.sum(axis=0, keepdims=True)
            m_ref[:, cols] = m_new
            al_ref[:, cols] = alpha
            p_ref[:, cols] = p.astype(BF16)

    def accumulate(kb):
        acc_ref[...] = al_ref[...] * acc_ref[...] + jnp.dot(vt_ref[0, 0, kb], p_ref[...],
                                                             preferred_element_type=F32)

    def tick(t, carry):
        accumulate(t - 2)
        softmax()
        scores(t)
        return carry

    scores(0)
    if nkb >= 2:
        softmax()
        scores(1)
        lax.fori_loop(2, nkb, tick, 0)
        accumulate(nkb - 2)
    softmax()
    accumulate(nkb - 1)

    lam = (jnp.exp(jnp.sum(lq1_ref[...] * lk1_ref[...], axis=-1, keepdims=True))
           - jnp.exp(jnp.sum(lq2_ref[...] * lk2_ref[...], axis=-1, keepdims=True)) + lam_init)
    o = acc_ref[...] / l_ref[...]
    ot = o[:, :tq] - lam * o[:, tq:]
    y = ot * lax.rsqrt(jnp.mean(ot * ot, axis=0, keepdims=True) + EPS) * g_ref[...]
    o_ref[...] = (y * (1.0 - lam_init)).T.astype(o_ref.dtype)


def _diff_attn(q_arr, q_col, k_all, vt_all, lams, ln_g, lam_init, b, seq, tq):
    nq = seq // tq
    nkb, tk = k_all.shape[1], k_all.shape[2]
    vec = pl.BlockSpec((1, DIFF_QK_DIM), lambda bi, h, qi: (0, 0))
    return pl.pallas_call(
        functools.partial(_diff_kernel, lam_init=lam_init),
        grid=(b, DIFF_HEADS, nq),
        in_specs=[pl.BlockSpec((tq, HEAD_W), lambda bi, h, qi: (bi * nq + qi, q_col + h)),
                  pl.BlockSpec((1, nkb, tk, HEAD_W), lambda bi, h, qi: (bi, 0, 0, h)),
                  pl.BlockSpec((1, 1, nkb, HEAD_W, tk), lambda bi, h, qi: (bi, h, 0, 0, 0)),
                  vec, vec, vec, vec,
                  pl.BlockSpec((HEAD_W, 1), lambda bi, h, qi: (0, 0))],
        out_specs=pl.BlockSpec((tq, HEAD_W), lambda bi, h, qi: (bi * nq + qi, h)),
        out_shape=jax.ShapeDtypeStruct((b * seq, DIFF_HEADS * HEAD_W), BF16),
        scratch_shapes=[pltpu.VMEM((HEAD_W, 2 * tq), BF16),
                        pltpu.VMEM((tk, 2 * tq), F32), pltpu.VMEM((tk, 2 * tq), BF16),
                        pltpu.VMEM((1, 2 * tq), F32),
                        pltpu.VMEM((1, 2 * tq), F32), pltpu.VMEM((1, 2 * tq), F32),
                        pltpu.VMEM((1, 2 * tq), F32), pltpu.VMEM((HEAD_W, 2 * tq), F32)],
        compiler_params=_params(("parallel", "parallel", "parallel")),
    )(q_arr, k_all, vt_all, *[v.reshape(1, DIFF_QK_DIM) for v in lams], ln_g.reshape(HEAD_W, 1))


def _largest_tile(n, limit):
    return max(t for t in range(HEAD_W, limit + 1, HEAD_W) if n % t == 0)


def _kv_layout(k_rows, v_rows, b, tk):
    lk = k_rows.shape[1]
    nkb = lk // tk
    k_all = k_rows.reshape(b, nkb, tk, DIFF_HEADS * HEAD_W)
    vt_all = v_rows.reshape(b, nkb, tk, DIFF_HEADS, HEAD_W).transpose(0, 3, 1, 4, 2)
    return k_all, vt_all


def _merge_kernel(oa_ref, ob_ref, oc_ref, g0_ref, g1_ref, g2_ref, pa_ref, pb_ref, pc_ref, wo_ref,
                  gp_ref, gate_ref, x_ref, o_ref):
    def branch(o_r, g_r, p_r):
        return _sigmoid(g_r[...].astype(F32)) * jnp.dot(o_r[...], p_r[...], preferred_element_type=F32)

    y = branch(oa_ref, g0_ref, pa_ref) + branch(ob_ref, g1_ref, pb_ref) + branch(oc_ref, g2_ref, pc_ref)
    mix = jnp.dot(y.astype(BF16), wo_ref[...], preferred_element_type=F32)
    r = mix * lax.rsqrt(jnp.mean(mix * mix, axis=-1, keepdims=True) + EPS) * gp_ref[...]
    o_ref[...] = x_ref[...] + gate_ref[0] * r


def _resident(shape):
    return pl.BlockSpec(shape, lambda i: (0,) * len(shape), pipeline_mode=pl.Buffered(1))


def _merge_out(oa, ob, oc, proj, gate_col, p_a, p_b, p_c, w_out, g_post, gate, x, seq, tm):
    n, d = x.shape
    w = oa.shape[1]
    tps = seq // tm

    def act():
        return pl.BlockSpec((tm, w), lambda i: (i, 0))

    def gcol(k):
        return pl.BlockSpec((tm, d), lambda i: (i, gate_col + k))

    return pl.pallas_call(
        _merge_kernel,
        grid=(n // tm,),
        in_specs=[act(), act(), act(), gcol(0), gcol(1), gcol(2),
                  _resident((w, d)), _resident((w, d)), _resident((w, d)), _resident((d, d)),
                  pl.BlockSpec((1, d), lambda i: (0, 0)),
                  pl.BlockSpec((1, 1, d), lambda i: (i // tps, 0, 0)),
                  pl.BlockSpec((tm, d), lambda i: (i, 0))],
        out_specs=pl.BlockSpec((tm, d), lambda i: (i, 0)),
        out_shape=jax.ShapeDtypeStruct((n, d), F32),
        compiler_params=_params(("parallel",)),
    )(oa, ob, oc, proj, proj, proj, p_a, p_b, p_c, w_out, g_post.reshape(1, d), gate, x)


def _up_gate_kernel(xp_ref, x_ref, xn_ref, g_ref, sh_ref, sc_ref, wa_ref, wb_ref, cwa_ref, cwb_ref,
                    cba_ref, cbb_ref, o_ref, h_ref, *, tps, m_split):
    tm = x_ref.shape[0]
    t = pl.program_id(0) % tps

    @pl.when(pl.program_id(1) == 0)
    def _():
        def norm_mod(x):
            y = x * lax.rsqrt(jnp.mean(x * x, axis=-1, keepdims=True) + EPS) * g_ref[...]
            return y * (1.0 + sc_ref[0]) + sh_ref[0]

        h_ref[HALO:HALO + tm, :] = norm_mod(x_ref[...]).astype(BF16)
        h_ref[0:HALO, :] = jnp.where(t == 0, 0.0, norm_mod(xp_ref[...])).astype(BF16)
        h_ref[HALO + tm:, :] = jnp.where(t == tps - 1, 0.0, norm_mod(xn_ref[...])).astype(BF16)

    n_ext = m_split + 2 * HALO
    for c in range(tm // m_split):
        r0 = c * m_split
        hc = h_ref[r0:r0 + n_ext, :]

        def half(w_ref, cw_ref, cb_ref):
            u = jnp.dot(hc, w_ref[...], preferred_element_type=F32)
            before = pltpu.roll(u, 1, axis=0)[HALO:HALO + m_split]
            after = pltpu.roll(u, n_ext - 1, axis=0)[HALO:HALO + m_split]
            return (before * cw_ref[0:1, :] + u[HALO:HALO + m_split] * cw_ref[1:2, :]
                    + after * cw_ref[2:3, :] + cb_ref[...])

        a = half(wa_ref, cwa_ref, cba_ref)
        b = half(wb_ref, cwb_ref, cbb_ref)
        o_ref[r0:r0 + m_split, :] = (a * _sigmoid(a) * b).astype(o_ref.dtype)


def _up_gate(x, g, shift, scale, w, conv_w, conv_b, seq, tm, tn, m_split):
    n, d = x.shape
    f2 = w.shape[1]
    f = f2 // 2
    kw = conv_w.shape[0]
    assert kw == 3 and f % tn == 0 and seq % tm == 0 and tm % m_split == 0 and tm % HALO == 0
    nc = f // tn
    tps = seq // tm
    r = tm // HALO
    last = n // HALO - 1
    cb = conv_b.reshape(1, f2)

    def wspec(o):
        return pl.BlockSpec((d, tn), lambda i, j: (0, j + o))

    def cwspec(o):
        return pl.BlockSpec((kw, tn), lambda i, j: (0, j + o))

    def cbspec(o):
        return pl.BlockSpec((1, tn), lambda i, j: (0, j + o))

    return pl.pallas_call(
        functools.partial(_up_gate_kernel, tps=tps, m_split=m_split),
        grid=(n // tm, nc),
        in_specs=[pl.BlockSpec((HALO, d), lambda i, j: (jnp.maximum(i * r - 1, 0), 0)),
                  pl.BlockSpec((tm, d), lambda i, j: (i, 0)),
                  pl.BlockSpec((HALO, d), lambda i, j: (jnp.minimum((i + 1) * r, last), 0)),
                  pl.BlockSpec((1, d), lambda i, j: (0, 0)),
                  pl.BlockSpec((1, 1, d), lambda i, j: (i // tps, 0, 0)),
                  pl.BlockSpec((1, 1, d), lambda i, j: (i // tps, 0, 0)),
                  wspec(0), wspec(nc), cwspec(0), cwspec(nc), cbspec(0), cbspec(nc)],
        out_specs=pl.BlockSpec((tm, tn), lambda i, j: (i, j)),
        out_shape=jax.ShapeDtypeStruct((n, f), BF16),
        scratch_shapes=[pltpu.VMEM((tm + 2 * HALO, d), BF16)],
        compiler_params=_params(("parallel", "arbitrary")),
    )(x, x, x, g.reshape(1, d), shift, scale, w, w, conv_w, conv_w, cb, cb)


def _down_kernel(h_ref, w_ref, gp_ref, gate_ref, x_ref, o_ref):
    y = jnp.dot(h_ref[...], w_ref[...], preferred_element_type=F32)
    r = y * lax.rsqrt(jnp.mean(y * y, axis=-1, keepdims=True) + EPS) * gp_ref[...]
    o_ref[...] = x_ref[...] + gate_ref[0] * r


def _down_norm_res(h, w, g_post, gate, x, seq, tm):
    n, d = x.shape
    f = h.shape[1]
    tps = seq // tm
    return pl.pallas_call(
        _down_kernel,
        grid=(n // tm,),
        in_specs=[pl.BlockSpec((tm, f), lambda i: (i, 0)),
                  _resident((f, d)),
                  pl.BlockSpec((1, d), lambda i: (0, 0)),
                  pl.BlockSpec((1, 1, d), lambda i: (i // tps, 0, 0)),
                  pl.BlockSpec((tm, d), lambda i: (i, 0))],
        out_specs=pl.BlockSpec((tm, d), lambda i: (i, 0)),
        out_shape=jax.ShapeDtypeStruct((n, d), F32),
        compiler_params=_params(("parallel",)),
    )(h, w, g_post.reshape(1, d), gate, x)


def kernel(x, c, ctx, c_ctx, w_mod, b_mod, g_pre_mix, w_in, na_rpb, conv_w, conv_b, conv_ln_g,
           conv_ln_b, lam_q1, lam_k1, lam_q2, lam_k2, diff_ln_g, p_a, p_b, p_c, w_out, g_post_mix,
           g_pre_ffn, w_up, ffn_conv_w, ffn_conv_b, w_down, g_post_ffn):
    b, seq, d = x.shape
    ctx_len = ctx.shape[1]
    depth = w_in.shape[0]
    na_w = NA_HEADS * HEAD_W
    conv_ch = conv_w.shape[2]
    qk_w = DIFF_HEADS * 2 * DIFF_QK_DIM
    u_off = 3 * na_w
    dq_off = u_off + 2 * conv_ch
    dv_off = dq_off + 2 * qk_w
    m_dv = dq_off
    m_gate = m_dv + DIFF_HEADS * HEAD_W
    na_q_col, na_k_col = 0, NA_HEADS
    tn = 1024
    assert m_gate % d == 0 and u_off % conv_ch == 0 and na_w == NA_HEADS * HEAD_W

    xs = x.reshape(b * seq, d)
    cs = ctx.reshape(b * ctx_len, d)
    cond = jnp.concatenate([c, c_ctx[None, :], jnp.zeros((8 - b - 1, d), F32)], axis=0)
    mods = _modulation(cond, w_mod, b_mod)
    rope_tabs = _rope_tables(seq)
    qscale = DIFF_QK_DIM ** -0.5 * math.log2(math.e)
    col_scale = jnp.concatenate([jnp.full((qk_w,), qscale, F32), jnp.ones((qk_w,), F32)])
    tk = 256
    tm_x, tm_c = 1024, ctx_len
    n_ck = seq // (NA_K_CHUNK * GRID_W)

    for l in range(depth):
        last = l == depth - 1
        lam_init = 0.8 - 0.6 * math.exp(-0.3 * l)
        mx = mods[l, :b].reshape(b, 6, 1, d)
        mc = jnp.broadcast_to(mods[l, b].reshape(1, 6, 1, d), (b, 6, 1, d))
        w_main = jnp.concatenate([w_in[l][:, :dq_off], w_in[l][:, dv_off:]], axis=1).astype(BF16)
        w_qk = w_in[l][:, dq_off:dv_off].astype(BF16)
        lams = (lam_q1[l], lam_k1[l], lam_q2[l], lam_k2[l])
        pa_l, pb_l, pc_l, wo_l = (p_a[l].astype(BF16), p_b[l].astype(BF16), p_c[l].astype(BF16),
                                  w_out[l].astype(BF16))
        w_up_l, w_down_l = w_up[l].astype(BF16), w_down[l].astype(BF16)

        proj = _norm_mod_matmul(xs, g_pre_mix[l], mx[:, 0], mx[:, 1], w_main, seq, tm_x, 2 * tn)
        qk = _norm_mod_matmul(xs, g_pre_mix[l], mx[:, 0], mx[:, 1], w_qk, seq, tm_x, 2 * tn,
                              col_scale=col_scale, rope_tabs=rope_tabs)
        cproj = _norm_mod_matmul(cs, g_pre_mix[l], mc[:, 0], mc[:, 1], w_main, ctx_len, tm_c, tn)
        cqk = _norm_mod_matmul(cs, g_pre_mix[l], mc[:, 0], mc[:, 1], w_qk, ctx_len, tm_c, tn,
                               col_scale=col_scale)
        bias = _na_bias_table(na_rpb[l])
        nv_t = (proj[:, 2 * na_w:3 * na_w].reshape(b, n_ck, NA_K_CHUNK * GRID_W, NA_HEADS, HEAD_W)
                .transpose(0, 3, 1, 4, 2))
        cnv_t = cproj[:, 2 * na_w:3 * na_w].reshape(b, ctx_len, NA_HEADS, HEAD_W).transpose(0, 2, 3, 1)
        oa = _na_latent(proj, cproj, nv_t, cnv_t, bias, b, seq, ctx_len, na_q_col, na_k_col)
        ob = _conformer(proj, conv_w[l], conv_b[l], conv_ln_g[l], conv_ln_b[l], seq, 512,
                        u_off // conv_ch, u_off // conv_ch + 1)
        ck = cqk[:, qk_w:].reshape(b, ctx_len, qk_w)
        cv = cproj[:, m_dv:m_gate].reshape(b, ctx_len, qk_w)
        k_rows = jnp.concatenate([qk[:, qk_w:].reshape(b, seq, qk_w), ck], axis=1)
        v_rows = jnp.concatenate([proj[:, m_dv:m_gate].reshape(b, seq, qk_w), cv], axis=1)
        k_all, vt_all = _kv_layout(k_rows, v_rows, b, _largest_tile(seq + ctx_len, DIFF_TK))
        oc = _diff_attn(qk, 0, k_all, vt_all, lams, diff_ln_g[l], lam_init, b, seq, 2048)
        if not last:
            oa_c = _dense_attn(cproj, b, ctx_len, na_q_col, na_k_col, 2 * NA_HEADS)
            ob_c = _conformer(cproj, conv_w[l], conv_b[l], conv_ln_g[l], conv_ln_b[l], ctx_len,
                              ctx_len, u_off // conv_ch, u_off // conv_ch + 1)
            kc_all, vtc_all = _kv_layout(ck, cv, b, tk)
            oc_c = _diff_attn(cqk, 0, kc_all, vtc_all, lams, diff_ln_g[l], lam_init, b, ctx_len, 256)
            cs = _merge_out(oa_c, ob_c, oc_c, cproj, m_gate // d, pa_l, pb_l, pc_l, wo_l,
                            g_post_mix[l], mc[:, 2], cs, ctx_len, 256)
        xs = _merge_out(oa, ob, oc, proj, m_gate // d, pa_l, pb_l, pc_l, wo_l, g_post_mix[l],
                        mx[:, 2], xs, seq, 256)

        hg = _up_gate(xs, g_pre_ffn[l], mx[:, 3], mx[:, 4], w_up_l, ffn_conv_w[l], ffn_conv_b[l],
                      seq, tm_x, 512, 1024)
        xs = _down_norm_res(hg, w_down_l, g_post_ffn[l], mx[:, 5], xs, seq, 512)
        if not last:
            hg = _up_gate(cs, g_pre_ffn[l], mc[:, 3], mc[:, 4], w_up_l, ffn_conv_w[l], ffn_conv_b[l],
                          ctx_len, tm_c, 512, tm_c)
            cs = _down_norm_res(hg, w_down_l, g_post_ffn[l], mc[:, 5], cs, ctx_len, 256)
    return xs.reshape(b, seq, d)
```
